```python
import math
import jax, jax.numpy as jnp
from jax import lax
import numpy as np

D_MODEL = 4096
BATCH = 4
SEQ = 4096
DEPTH = 4
DEC_BATCH = 8
DEC_SEQ = 16
PAST_LEN = 1024

CHUNK = 64
N_HEADS_A = 16
HEAD_K = 128
HEAD_V = 128
KEY_A = N_HEADS_A * HEAD_K
VAL_A = N_HEADS_A * HEAD_V
QKV_A = 2 * KEY_A + VAL_A
CONV_A = 4
W_LRU = D_MODEL // 2
N_BLOCKS_LRU = 16
BLOCK_LRU = W_LRU // N_BLOCKS_LRU
CONV_LRU = 4
C_LRU = 8.0
D_FF = 3 * D_MODEL
CONV_FFN = 3
EPS = 1e-6

_O1 = QKV_A
_O2 = _O1 + N_HEADS_A
_O3 = _O2 + N_HEADS_A
_O4 = _O3 + VAL_A
_O5 = _O4 + W_LRU
_O6 = _O5 + W_LRU
_O7 = _O6 + D_MODEL
N_IN = _O7 + D_MODEL
IN_SPLITS = (_O1, _O2, _O3, _O4, _O5, _O6, _O7)

kernel_name = 'hybrid_gdn_rglru_convffn_stream_step'


def rms_norm(x, g):
    x32 = x.astype(jnp.float32)
    y = x32 * lax.rsqrt(jnp.mean(x32 * x32, axis=-1, keepdims=True) + EPS)
    return (y * g.astype(jnp.float32)).astype(x.dtype)


def l2_norm(x):
    return x * lax.rsqrt(jnp.sum(x * x, axis=-1, keepdims=True) + EPS)


def causal_dwconv(x, buf, w):
    k = w.shape[0]
    t = x.shape[1]
    xp = jnp.concatenate([buf.astype(x.dtype), x], axis=1)
    y = sum(xp[:, j:j + t] * w[j] for j in range(k))
    return y, xp[:, t:]


def gated_delta_rule(q, k, v, g, beta, s0):
    b, t, h, dk = q.shape
    dv = v.shape[-1]
    L = min(CHUNK, t)
    n = -(-t // L)
    pad = n * L - t

    def to_chunks(a):
        a = a.astype(jnp.float32)
        a = jnp.pad(a, [(0, 0), (0, pad)] + [(0, 0)] * (a.ndim - 2))
        a = a.reshape((b, n, L) + a.shape[2:])
        return jnp.moveaxis(a, 3, 1)

    qc, kc, vc, gc, bc = (to_chunks(a) for a in (q, k, v, g, beta))
    G = jnp.cumsum(gc, axis=-1)
    causal = jnp.tril(jnp.ones((L, L), bool))
    strict = jnp.tril(jnp.ones((L, L), bool), -1)
    diff = G[..., :, None] - G[..., None, :]
    gamma = jnp.where(causal, jnp.exp(jnp.where(causal, diff, 0.0)), 0.0)
    kb = kc * bc[..., None]
    a_mat = jnp.where(strict, jnp.einsum('bhnid,bhnjd->bhnij', kb, kc) * gamma, 0.0)
    m = a_mat + jnp.eye(L, dtype=jnp.float32)
    rhs = jnp.concatenate([vc * bc[..., None], kb * jnp.exp(G)[..., None]], axis=-1)
    sol = lax.linalg.triangular_solve(m, rhs, left_side=True, lower=True, unit_diagonal=True)
    u, w = sol[..., :dv], sol[..., dv:]
    qk = jnp.einsum('bhnid,bhnjd->bhnij', qc, kc) * gamma
    q_dec = qc * jnp.exp(G)[..., None]
    k_dec = kc * jnp.exp(G[..., -1:] - G)[..., None]
    chunk_dec = jnp.exp(G[..., -1])

    def step(s, xs):
        qk_i, qd_i, kd_i, u_i, w_i, cd_i = xs
        v_new = u_i - jnp.einsum('bhld,bhde->bhle', w_i, s)
        o_i = jnp.einsum('bhld,bhde->bhle', qd_i, s) + jnp.einsum('bhij,bhje->bhie', qk_i, v_new)
        s = s * cd_i[..., None, None] + jnp.einsum('bhld,bhle->bhde', kd_i, v_new)
        return s, o_i

    xs = tuple(jnp.moveaxis(a, 2, 0) for a in (qk, q_dec, k_dec, u, w, chunk_dec))
    s_final, o = lax.scan(step, s0.astype(jnp.float32), xs)
    o = jnp.transpose(o, (1, 0, 3, 2, 4)).reshape(b, n * L, h, dv)[:, :t]
    return o, s_final


def _lin_combine(left, right):
    a1, b1 = left
    a2, b2 = right
    return a1 * a2, a2 * b1 + b2


def mixer_block(h, buf_qkv, s_delta, buf_lru, h_lru, first, w_in, conv_qkv_w, a_log, dt_bias, gnorm_a, w_ao,
                conv_lru_w, conv_lru_b, lru_wr, lru_br, lru_wi, lru_bi, lru_lambda, w_bo, w_o):
    b, t, _ = h.shape
    f32 = jnp.float32
    proj = h @ w_in
    qkv, a_raw, b_raw, z, xb, gb, ga_logit, gb_logit = jnp.split(proj, IN_SPLITS, axis=-1)

    qkv_c, buf_qkv_new = causal_dwconv(qkv, buf_qkv, conv_qkv_w)
    qkv_c = jax.nn.silu(qkv_c)
    q, k, v = jnp.split(qkv_c, (KEY_A, 2 * KEY_A), axis=-1)
    q = l2_norm(q.reshape(b, t, N_HEADS_A, HEAD_K).astype(f32)) * (HEAD_K ** -0.5)
    k = l2_norm(k.reshape(b, t, N_HEADS_A, HEAD_K).astype(f32))
    v = v.reshape(b, t, N_HEADS_A, HEAD_V)
    beta = jax.nn.sigmoid(b_raw.astype(f32))
    g = -jnp.exp(a_log.astype(f32)) * jax.nn.softplus(a_raw.astype(f32) + dt_bias.astype(f32))
    o, s_delta_new = gated_delta_rule(q, k, v, g, beta, s_delta)
    zh = z.reshape(b, t, N_HEADS_A, HEAD_V).astype(f32)
    o = o * lax.rsqrt(jnp.mean(o * o, axis=-1, keepdims=True) + EPS) * gnorm_a.astype(f32) * jax.nn.silu(zh)
    y_a = o.reshape(b, t, VAL_A).astype(h.dtype) @ w_ao

    xc, buf_lru_new = causal_dwconv(xb, buf_lru, conv_lru_w)
    xc = xc + conv_lru_b
    xh = xc.reshape(b, t, N_BLOCKS_LRU, BLOCK_LRU)
    r = jax.nn.sigmoid((jnp.einsum('btnd,nde->btne', xh, lru_wr).reshape(b, t, W_LRU) + lru_br).astype(f32))
    i = jax.nn.sigmoid((jnp.einsum('btnd,nde->btne', xh, lru_wi).reshape(b, t, W_LRU) + lru_bi).astype(f32))
    log_a = -C_LRU * r * jax.nn.softplus(-lru_lambda.astype(f32))
    a = jnp.exp(log_a)
    mult = jnp.sqrt(-jnp.expm1(2.0 * log_a))
    if first:
        reset = (jnp.arange(t) == 0)[None, :, None]
        a = jnp.where(reset, 0.0, a)
        mult = jnp.where(reset, 1.0, mult)
    u = mult * i * xc.astype(f32)
    u = u.at[:, 0].add(a[:, 0] * h_lru.astype(f32))
    _, hs = lax.associative_scan(_lin_combine, (a, u), axis=1)
    h_lru_new = hs[:, -1]
    y_b = (hs.astype(h.dtype) * jax.nn.gelu(gb)) @ w_bo

    mixed = jax.nn.sigmoid(ga_logit) * y_a + jax.nn.sigmoid(gb_logit) * y_b
    return mixed @ w_o, s_delta_new, buf_qkv_new, h_lru_new, buf_lru_new


def conv_ffn(h, buf_ffn, w_up, conv_ffn_w, conv_ffn_b, w_down):
    u_g, u_v = jnp.split(h @ w_up, 2, axis=-1)
    u_g, buf_new = causal_dwconv(u_g, buf_ffn, conv_ffn_w)
    return (jax.nn.gelu(u_g + conv_ffn_b) * u_v) @ w_down, buf_new


def trunk(x, c, st_delta, st_qkv, st_lru, st_lruconv, st_ffn, first, weights):
    (norm1_g, w_ada, b_ada, w_in, conv_qkv_w, a_log, dt_bias, gnorm_a, w_ao, conv_lru_w, conv_lru_b,
     lru_wr, lru_br, lru_wi, lru_bi, lru_lambda, w_bo, w_o, norm2_g, w_up, conv_ffn_w, conv_ffn_b,
     w_down, final_g) = weights
    dt = x.dtype
    n_delta, n_qkv, n_lru, n_lruconv, n_ffn = [], [], [], [], []
    for l in range(DEPTH):
        mod = jax.nn.silu(c) @ w_ada[l] + b_ada[l]
        sh1, sc1, gt1, sh2, sc2, gt2 = [m[:, None, :] for m in jnp.split(mod, 6, axis=-1)]
        h = rms_norm(x, norm1_g[l]) * (1.0 + sc1) + sh1
        out, sd, bq, hl, bl = mixer_block(
            h, st_qkv[l], st_delta[l], st_lruconv[l], st_lru[l], first, w_in[l], conv_qkv_w[l], a_log[l],
            dt_bias[l], gnorm_a[l], w_ao[l], conv_lru_w[l], conv_lru_b[l], lru_wr[l], lru_br[l], lru_wi[l],
            lru_bi[l], lru_lambda[l], w_bo[l], w_o[l])
        x = x + gt1 * out
        h = rms_norm(x, norm2_g[l]) * (1.0 + sc2) + sh2
        y, bf = conv_ffn(h, st_ffn[l], w_up[l], conv_ffn_w[l], conv_ffn_b[l], w_down[l])
        x = x + gt2 * y
        n_delta.append(sd.astype(dt))
        n_qkv.append(bq.astype(dt))
        n_lru.append(hl.astype(dt))
        n_lruconv.append(bl.astype(dt))
        n_ffn.append(bf.astype(dt))
    y = rms_norm(x, final_g)
    return (y, jnp.stack(n_delta), jnp.stack(n_qkv), jnp.stack(n_lru), jnp.stack(n_lruconv), jnp.stack(n_ffn))


def setup_inputs(seed: int = 0) -> dict:
    key = jax.random.key(seed)
    ks = iter(jax.random.split(key, 48))
    f32 = jnp.float32

    def nrm(shape, s):
        return jax.random.normal(next(ks), shape, f32) * s

    def gain(shape):
        return 1.0 + nrm(shape, 0.01)

    x_prompt = nrm((BATCH, SEQ, D_MODEL), 1.0)
    x_sample = nrm((DEC_BATCH, DEC_SEQ, D_MODEL), 1.0)
    state_delta = nrm((DEPTH, DEC_BATCH, N_HEADS_A, HEAD_K, HEAD_V), 0.5)
    state_qkv_conv = nrm((DEPTH, DEC_BATCH, CONV_A - 1, QKV_A), 1.0)
    state_lru = nrm((DEPTH, DEC_BATCH, W_LRU), 0.5)
    state_lru_conv = nrm((DEPTH, DEC_BATCH, CONV_LRU - 1, W_LRU), 1.0)
    state_ffn_conv = nrm((DEPTH, DEC_BATCH, CONV_FFN - 1, D_FF), 1.0)
    c_prompt = nrm((BATCH, D_MODEL), 1.0)
    c_sample = nrm((DEC_BATCH, D_MODEL), 1.0)

    norm1_g = gain((DEPTH, D_MODEL))
    w_ada = nrm((DEPTH, D_MODEL, 6 * D_MODEL), 0.5 * D_MODEL ** -0.5)
    b_ada = nrm((DEPTH, 6 * D_MODEL), 0.01)
    w_in = nrm((DEPTH, D_MODEL, N_IN), D_MODEL ** -0.5)
    conv_qkv_w = nrm((DEPTH, CONV_A, QKV_A), CONV_A ** -0.5)
    a_log = jnp.log(jax.random.uniform(next(ks), (DEPTH, N_HEADS_A), f32, 1.0, 16.0))
    dt0 = jnp.exp(jax.random.uniform(next(ks), (DEPTH, N_HEADS_A), f32, math.log(1e-3), math.log(0.1)))
    dt_bias = dt0 + jnp.log(-jnp.expm1(-dt0))
    gnorm_a = gain((DEPTH, HEAD_V))
    w_ao = nrm((DEPTH, VAL_A, D_MODEL), VAL_A ** -0.5)
    conv_lru_w = nrm((DEPTH, CONV_LRU, W_LRU), CONV_LRU ** -0.5)
    conv_lru_b = nrm((DEPTH, W_LRU), 0.01)
    lru_wr = nrm((DEPTH, N_BLOCKS_LRU, BLOCK_LRU, BLOCK_LRU), BLOCK_LRU ** -0.5)
    lru_br = nrm((DEPTH, W_LRU), 0.01)
    lru_wi = nrm((DEPTH, N_BLOCKS_LRU, BLOCK_LRU, BLOCK_LRU), BLOCK_LRU ** -0.5)
    lru_bi = nrm((DEPTH, W_LRU), 0.01)
    a_c = jax.random.uniform(next(ks), (DEPTH, W_LRU), f32, 0.9, 0.999)
    s_l = a_c ** (1.0 / C_LRU)
    lru_lambda = jnp.log(s_l) - jnp.log1p(-s_l)
    w_bo = nrm((DEPTH, W_LRU, D_MODEL), W_LRU ** -0.5)
    w_o = nrm((DEPTH, D_MODEL, D_MODEL), D_MODEL ** -0.5)
    norm2_g = gain((DEPTH, D_MODEL))
    w_up = nrm((DEPTH, D_MODEL, 2 * D_FF), D_MODEL ** -0.5)
    conv_ffn_w = nrm((DEPTH, CONV_FFN, D_FF), CONV_FFN ** -0.5)
    conv_ffn_b = nrm((DEPTH, D_FF), 0.01)
    w_down = nrm((DEPTH, D_FF, D_MODEL), D_FF ** -0.5)
    final_g = gain((D_MODEL,))
    return {'x_prompt': x_prompt, 'x_sample': x_sample, 'state_delta': state_delta,
            'state_qkv_conv': state_qkv_conv, 'state_lru': state_lru, 'state_lru_conv': state_lru_conv,
            'state_ffn_conv': state_ffn_conv, 'c_prompt': c_prompt, 'c_sample': c_sample,
            'norm1_g': norm1_g, 'w_ada': w_ada, 'b_ada': b_ada, 'w_in': w_in, 'conv_qkv_w': conv_qkv_w,
            'a_log': a_log, 'dt_bias': dt_bias, 'gnorm_a': gnorm_a, 'w_ao': w_ao, 'conv_lru_w': conv_lru_w,
            'conv_lru_b': conv_lru_b, 'lru_wr': lru_wr, 'lru_br': lru_br, 'lru_wi': lru_wi, 'lru_bi': lru_bi,
            'lru_lambda': lru_lambda, 'w_bo': w_bo, 'w_o': w_o, 'norm2_g': norm2_g, 'w_up': w_up,
            'conv_ffn_w': conv_ffn_w, 'conv_ffn_b': conv_ffn_b, 'w_down': w_down, 'final_g': final_g}


def reference(x_prompt, x_sample, state_delta, state_qkv_conv, state_lru, state_lru_conv, state_ffn_conv,
              c_prompt, c_sample, norm1_g, w_ada, b_ada, w_in, conv_qkv_w, a_log, dt_bias, gnorm_a, w_ao,
              conv_lru_w, conv_lru_b, lru_wr, lru_br, lru_wi, lru_bi, lru_lambda, w_bo, w_o, norm2_g, w_up,
              conv_ffn_w, conv_ffn_b, w_down, final_g):
    weights = (norm1_g, w_ada, b_ada, w_in, conv_qkv_w, a_log, dt_bias, gnorm_a, w_ao, conv_lru_w, conv_lru_b,
               lru_wr, lru_br, lru_wi, lru_bi, lru_lambda, w_bo, w_o, norm2_g, w_up, conv_ffn_w, conv_ffn_b,
               w_down, final_g)
    bp = x_prompt.shape[0]
    dt = x_prompt.dtype
    z_delta = jnp.zeros((DEPTH, bp, N_HEADS_A, HEAD_K, HEAD_V), jnp.float32)
    z_qkv = jnp.zeros((DEPTH, bp, CONV_A - 1, QKV_A), dt)
    z_lru = jnp.zeros((DEPTH, bp, W_LRU), jnp.float32)
    z_lruconv = jnp.zeros((DEPTH, bp, CONV_LRU - 1, W_LRU), dt)
    z_ffn = jnp.zeros((DEPTH, bp, CONV_FFN - 1, D_FF), dt)
    y_prompt, p_delta, p_qkv, p_lru, p_lruconv, p_ffn = trunk(
        x_prompt, c_prompt, z_delta, z_qkv, z_lru, z_lruconv, z_ffn, True, weights)
    y_sample, s_delta, s_qkv, s_lru, s_lruconv, s_ffn = trunk(
        x_sample, c_sample, state_delta, state_qkv_conv, state_lru, state_lru_conv, state_ffn_conv, False,
        weights)
    return (y_prompt, y_sample, p_delta, p_qkv, p_lru, p_lruconv, p_ffn, s_delta, s_qkv, s_lru, s_lruconv, s_ffn)
```

```python
import functools

import jax
import jax.numpy as jnp
from jax import lax
from jax.experimental import pallas as pl
from jax.experimental.pallas import tpu as pltpu

F32 = jnp.float32
BF16 = jnp.bfloat16

LANE = 128
SUBLANE = 8
VMEM_LIMIT_BYTES = 56 * 1024 * 1024

EPS = 1e-6
C_LRU = 8.0
HEAD_DIM = 128
DELTA_CHUNK = 128
HALO = SUBLANE


def _params(*sem):
    return pltpu.CompilerParams(dimension_semantics=sem, vmem_limit_bytes=VMEM_LIMIT_BYTES)


def _tile(dim, pref):
    if dim <= pref:
        return dim
    for t in range(pref - pref % LANE, 0, -LANE):
        if dim % t == 0:
            return t
    raise ValueError((dim, pref))


def _silu(x):
    return x * jax.nn.sigmoid(x)


def _softplus(x):
    return jnp.maximum(x, 0.0) + jnp.log1p(jnp.exp(-jnp.abs(x)))


def _bdot(a, b):
    return jnp.dot(a.astype(BF16), b.astype(BF16), preferred_element_type=F32)


def _ada_kernel(c_ref, w_ref, b_ref, o_ref):
    o_ref[0] = _bdot(_silu(c_ref[...]), w_ref[0]) + b_ref[0]


def _ada_mod(c_all, w_ada, b_ada):
    rows, d = c_all.shape
    depth, _, n = w_ada.shape
    tn = _tile(n, 512)
    return pl.pallas_call(
        _ada_kernel,
        out_shape=jax.ShapeDtypeStruct((depth, rows, n), F32),
        grid=(depth, n // tn),
        in_specs=[
            pl.BlockSpec((rows, d), lambda l, j: (0, 0)),
            pl.BlockSpec((1, d, tn), lambda l, j: (l, 0, j)),
            pl.BlockSpec((1, 1, tn), lambda l, j: (l, 0, j)),
        ],
        out_specs=pl.BlockSpec((1, rows, tn), lambda l, j: (l, 0, j)),
        compiler_params=_params("parallel", "parallel"),
        name="ada_mod",
    )(c_all, w_ada, b_ada.reshape(depth, 1, n))


def _norm_kernel(*refs, modulate):
    if modulate:
        x_ref, g_ref, sc_ref, sh_ref, o_ref = refs
    else:
        x_ref, g_ref, o_ref = refs
    x = x_ref[0]
    y = x * lax.rsqrt(jnp.mean(x * x, axis=-1, keepdims=True) + EPS) * g_ref[...]
    if modulate:
        y = y * (1.0 + sc_ref[0]) + sh_ref[0]
    o_ref[0] = y.astype(o_ref.dtype)


def _rms_norm(x, g, sc=None, sh=None, *, out_dtype):
    b, t, d = x.shape
    tt = _tile(t, 256)
    modulate = sc is not None
    row = pl.BlockSpec((1, 1, d), lambda i, j: (i, 0, 0))
    in_specs = [pl.BlockSpec((1, tt, d), lambda i, j: (i, j, 0)), pl.BlockSpec((1, d), lambda i, j: (0, 0))]
    args = [x, g.reshape(1, d)]
    if modulate:
        in_specs += [row, row]
        args += [sc, sh]
    return pl.pallas_call(
        functools.partial(_norm_kernel, modulate=modulate),
        out_shape=jax.ShapeDtypeStruct((b, t, d), out_dtype),
        grid=(b, t // tt),
        in_specs=in_specs,
        out_specs=pl.BlockSpec((1, tt, d), lambda i, j: (i, j, 0)),
        compiler_params=_params("parallel", "parallel"),
        name="rms_norm_mod" if modulate else "rms_norm",
    )(*args)


def _matmul_kernel(*refs, n_extra, nk, epilogue):
    a_ref, b_ref = refs[:2]
    extra = refs[2:2 + n_extra]
    o_ref = refs[2 + n_extra]

    def finish(acc):
        o_ref[...] = epilogue(acc, *[e[...] for e in extra]).astype(o_ref.dtype)

    part = jnp.dot(a_ref[...], b_ref[...], preferred_element_type=F32)
    if nk == 1:
        finish(part)
        return
    acc_ref = refs[3 + n_extra]
    k = pl.program_id(2)

    @pl.when(k == 0)
    def _():
        acc_ref[...] = part

    @pl.when(k > 0)
    def _():
        acc_ref[...] += part

    @pl.when(k == nk - 1)
    def _():
        finish(acc_ref[...])


def _matmul(a, b, *, out_dtype, tm, tn, tk, epilogue=None, extras=(), name):
    m, k = a.shape
    _, n = b.shape
    tm, tn, tk = _tile(m, tm), _tile(n, tn), _tile(k, tk)
    nk = k // tk
    if epilogue is None:
        epilogue = lambda acc: acc
    in_specs = [pl.BlockSpec((tm, tk), lambda i, j, kk: (i, kk)), pl.BlockSpec((tk, tn), lambda i, j, kk: (kk, j))]
    for _, shape, imap in extras:
        in_specs.append(pl.BlockSpec(shape, functools.partial(lambda i, j, kk, f: f(i, j), f=imap)))
    return pl.pallas_call(
        functools.partial(_matmul_kernel, n_extra=len(extras), nk=nk, epilogue=epilogue),
        out_shape=jax.ShapeDtypeStruct((m, n), out_dtype),
        grid=(m // tm, n // tn, nk),
        in_specs=in_specs,
        out_specs=pl.BlockSpec((tm, tn), lambda i, j, kk: (i, j)),
        scratch_shapes=[pltpu.VMEM((tm, tn), F32)] if nk > 1 else [],
        compiler_params=_params("parallel", "parallel", "arbitrary"),
        name=name,
    )(a, b, *[e[0] for e in extras])


def _merge_epilogue(acc, y_a, ga, gb):
    return jax.nn.sigmoid(ga) * y_a + jax.nn.sigmoid(gb) * acc


def _gate_extra(gate, t, tm, tn):
    if tm <= t:
        return (gate, (1, 1, tn), lambda i, j: ((i * tm) // t, 0, j))
    assert tm % t == 0
    return (gate, (tm // t, 1, tn), lambda i, j: (i, 0, j))


def _residual_epilogue_multi(acc, x, gate, *, t):
    nb = gate.shape[0]
    if nb == 1:
        return x + gate[0] * acc
    parts = [x[s * t:(s + 1) * t] + gate[s] * acc[s * t:(s + 1) * t] for s in range(nb)]
    return jnp.concatenate(parts, axis=0)


def _conv_from_scratch(xp_ref, w_ref, rows, base=0):
    kw = w_ref.shape[0]
    y = None
    for j in range(kw):
        term = w_ref[j:j + 1, :] * xp_ref[pl.ds(base + HALO - (kw - 1) + j, rows), :]
        y = term if y is None else y + term
    return y


def _qkv_conv_kernel(x_ref, st_ref, w_ref, o_ref, nb_ref, xp_ref, *, tt, n_qk_tiles, q_tiles):
    c, t = pl.program_id(1), pl.program_id(2)
    nt = pl.num_programs(2)
    hl = w_ref.shape[0] - 1

    @pl.when(t == 0)
    def _():
        xp_ref[HALO - hl:HALO, :] = st_ref[0]

    @pl.when(t > 0)
    def _():
        xp_ref[HALO - hl:HALO, :] = xp_ref[HALO + tt - hl:HALO + tt, :]

    xp_ref[HALO:HALO + tt, :] = x_ref[0]
    y = _silu(_conv_from_scratch(xp_ref, w_ref, tt))
    is_qk = c < n_qk_tiles
    scale = jnp.where(c < q_tiles, HEAD_DIM ** -0.5, 1.0).astype(F32)
    for hh in range(y.shape[1] // HEAD_DIM):
        blk = y[:, hh * HEAD_DIM:(hh + 1) * HEAD_DIM]
        inv = lax.rsqrt(jnp.sum(blk * blk, axis=-1, keepdims=True) + EPS) * scale
        o_ref[0, :, hh * HEAD_DIM:(hh + 1) * HEAD_DIM] = blk * jnp.where(is_qk, inv, 1.0)

    @pl.when(t == nt - 1)
    def _():
        nb_ref[0] = xp_ref[HALO + tt - hl:HALO + tt, :]


def _qkv_conv(proj, state, w, *, key_a, qkv_a):
    b, t, _ = proj.shape
    tt = _tile(t, 256)
    tc = _tile(key_a, 512)
    kw = w.shape[0]
    return pl.pallas_call(
        functools.partial(_qkv_conv_kernel, tt=tt, n_qk_tiles=2 * key_a // tc, q_tiles=key_a // tc),
        out_shape=(jax.ShapeDtypeStruct((b, t, qkv_a), F32), jax.ShapeDtypeStruct((b, kw - 1, qkv_a), F32)),
        grid=(b, qkv_a // tc, t // tt),
        in_specs=[
            pl.BlockSpec((1, tt, tc), lambda i, c, j: (i, j, c)),
            pl.BlockSpec((1, kw - 1, tc), lambda i, c, j: (i, 0, c)),
            pl.BlockSpec((kw, tc), lambda i, c, j: (0, c)),
        ],
        out_specs=(
            pl.BlockSpec((1, tt, tc), lambda i, c, j: (i, j, c)),
            pl.BlockSpec((1, kw - 1, tc), lambda i, c, j: (i, 0, c)),
        ),
        scratch_shapes=[pltpu.VMEM((HALO + tt, tc), F32)],
        compiler_params=_params("parallel", "parallel", "arbitrary"),
        name="qkv_conv",
    )(proj, state, w)


def _delta_prep_kernel(q_ref, k_ref, v_ref, ab_ref, alog_ref, dtb_ref, u_ref, wq_ref, qkk_ref, cd_ref,
                       *, n_heads, t_valid):
    L = DELTA_CHUNK
    t = pl.program_id(1)
    ab = ab_ref[0]
    valid = (t * L + lax.broadcasted_iota(jnp.int32, ab.shape, 0)) < t_valid
    g = jnp.where(valid, -jnp.exp(alog_ref[...]) * _softplus(ab + dtb_ref[...]), 0.0)
    beta = jnp.where(valid, jax.nn.sigmoid(ab), 0.0)
    g_t = g.T
    lane = lax.broadcasted_iota(jnp.int32, ab.shape, 1)
    ii = lax.broadcasted_iota(jnp.int32, (L, L), 0)
    jj = lax.broadcasted_iota(jnp.int32, (L, L), 1)
    causal = ii >= jj
    strict = ii > jj
    n_levels = L.bit_length() - 1
    level_masks = [((ii >> s) ^ (jj >> s)) == 1 for s in range(n_levels)]

    for hh in range(n_heads):
        hs = slice(hh * HEAD_DIM, (hh + 1) * HEAD_DIM)
        g_col = jnp.sum(jnp.where(lane == hh, g, 0.0), axis=1, keepdims=True)
        b_col = jnp.sum(jnp.where(lane == n_heads + hh, beta, 0.0), axis=1, keepdims=True)
        g_row = g_t[hh:hh + 1, :]
        cum_col = jnp.sum(jnp.where(causal, g_row, 0.0), axis=1, keepdims=True)
        cum_row = jnp.sum(jnp.where(ii <= jj, g_col, 0.0), axis=0, keepdims=True)
        gamma = jnp.where(causal, jnp.exp(jnp.where(causal, cum_col - cum_row, 0.0)), 0.0)
        g_last = jnp.sum(g_col, axis=0, keepdims=True)
        e_col = jnp.exp(cum_col)

        q, k, v = q_ref[0, :, hs], k_ref[0, :, hs], v_ref[0, :, hs]
        kb = k * b_col
        kq = lax.dot_general(jnp.concatenate([kb, q], axis=0).astype(BF16), k.astype(BF16),
                             (((1,), (1,)), ((), ())), preferred_element_type=F32)
        a_mat = jnp.where(strict, kq[:L] * gamma, 0.0)
        qk = kq[L:] * gamma

        x = -jnp.where(level_masks[0], a_mat, 0.0)
        for s in range(1, n_levels):
            a_s = jnp.where(level_masks[s], a_mat, 0.0)
            p = a_s + _bdot(x, a_s)
            x = x - (p + _bdot(p, x))

        rhs = jnp.concatenate([v * b_col, kb * e_col], axis=1)
        sol = rhs + _bdot(x, rhs)
        u_ref[0, :, hs] = sol[:, :HEAD_DIM]
        wq_ref[0, 0:L, hs] = sol[:, HEAD_DIM:].astype(BF16)
        wq_ref[0, L:2 * L, hs] = (q * e_col).astype(BF16)
        qkk_ref[0, 0:L, hs] = qk.astype(BF16)
        qkk_ref[0, L:2 * L, hs] = (k * jnp.exp(g_last - cum_col)).T.astype(BF16)
        cd_ref[0, 0, :, hs] = jnp.broadcast_to(jnp.exp(g_last), (SUBLANE, HEAD_DIM))


def _delta_prep(qkv, ab, a_log, dt_bias, *, n_heads, t_valid):
    b, t, _ = qkv.shape
    L = DELTA_CHUNK
    hw = n_heads * HEAD_DIM
    nc = t // L
    pad = lambda a: jnp.pad(a.reshape(1, n_heads), ((0, 0), (0, LANE - n_heads)))
    tok = lambda off: pl.BlockSpec((1, L, hw), lambda i, j: (i, j, off))
    row = pl.BlockSpec((1, LANE), lambda i, j: (0, 0))
    return pl.pallas_call(
        functools.partial(_delta_prep_kernel, n_heads=n_heads, t_valid=t_valid),
        out_shape=(
            jax.ShapeDtypeStruct((b, t, hw), F32),
            jax.ShapeDtypeStruct((b, 2 * t, hw), BF16),
            jax.ShapeDtypeStruct((b, 2 * t, hw), BF16),
            jax.ShapeDtypeStruct((b, nc, SUBLANE, hw), F32),
        ),
        grid=(b, nc),
        in_specs=[tok(0), tok(1), tok(2), pl.BlockSpec((1, L, LANE), lambda i, j: (i, j, 0)), row, row],
        out_specs=(
            pl.BlockSpec((1, L, hw), lambda i, j: (i, j, 0)),
            pl.BlockSpec((1, 2 * L, hw), lambda i, j: (i, j, 0)),
            pl.BlockSpec((1, 2 * L, hw), lambda i, j: (i, j, 0)),
            pl.BlockSpec((1, 1, SUBLANE, hw), lambda i, j: (i, j, 0, 0)),
        ),
        compiler_params=_params("parallel", "parallel"),
        name="delta_prep",
    )(qkv, qkv, qkv, ab, pad(a_log), pad(dt_bias))


def _delta_scan_kernel(u_ref, wq_ref, qkk_ref, cd_ref, z_ref, gn_ref, s0_ref, o_ref, so_ref, s_ref,
                       *, hb, nc, rows_out):
    L = DELTA_CHUNK
    t = pl.program_id(2)

    @pl.when(t == 0)
    def _():
        s_ref[...] = s0_ref[0]

    for c in range(nc):
        for hh in range(hb):
            hs = slice(hh * HEAD_DIM, (hh + 1) * HEAD_DIM)
            s = s_ref[hh]
            x1 = jnp.dot(wq_ref[0, 2 * c * L:2 * (c + 1) * L, hs], s.astype(BF16), preferred_element_type=F32)
            v_new = u_ref[0, c * L:(c + 1) * L, hs] - x1[:L]
            x2 = jnp.dot(qkk_ref[0, 2 * c * L:2 * (c + 1) * L, hs], v_new.astype(BF16),
                         preferred_element_type=F32)
            s_ref[hh] = s * cd_ref[0, c, 0:1, hs] + x2[L:]
            o = (x1[L:] + x2[:L])[:rows_out]
            z = z_ref[0, c * rows_out:(c + 1) * rows_out, hs]
            o = o * lax.rsqrt(jnp.mean(o * o, axis=-1, keepdims=True) + EPS) * gn_ref[...] * _silu(z)
            o_ref[0, c * rows_out:(c + 1) * rows_out, hs] = o.astype(o_ref.dtype)

    @pl.when(t == pl.num_programs(2) - 1)
    def _():
        so_ref[0] = s_ref[...]


def _delta_scan(u, wq, qkk, cd, proj, z_off, gnorm, s0, *, t_real):
    b, t, hw = u.shape
    L = DELTA_CHUNK
    n_heads = hw // HEAD_DIM
    hb = min(n_heads, 4)
    hbw = hb * HEAD_DIM
    assert z_off % hbw == 0
    nc = min(t // L, 4)
    rows_out = L if t_real == t else t_real
    assert t_real == t or t == L
    nt = t // (nc * L)
    return pl.pallas_call(
        functools.partial(_delta_scan_kernel, hb=hb, nc=nc, rows_out=rows_out),
        out_shape=(jax.ShapeDtypeStruct((b, t_real, hw), BF16), jax.ShapeDtypeStruct(s0.shape, F32)),
        grid=(b, n_heads // hb, nt),
        in_specs=[
            pl.BlockSpec((1, nc * L, hbw), lambda i, h, j: (i, j, h)),
            pl.BlockSpec((1, 2 * nc * L, hbw), lambda i, h, j: (i, j, h)),
            pl.BlockSpec((1, 2 * nc * L, hbw), lambda i, h, j: (i, j, h)),
            pl.BlockSpec((1, nc, SUBLANE, hbw), lambda i, h, j: (i, j, 0, h)),
            pl.BlockSpec((1, nc * rows_out, hbw), lambda i, h, j: (i, j, z_off // hbw + h)),
            pl.BlockSpec((1, HEAD_DIM), lambda i, h, j: (0, 0)),
            pl.BlockSpec((1, hb, HEAD_DIM, HEAD_DIM), lambda i, h, j: (i, h, 0, 0)),
        ],
        out_specs=(
            pl.BlockSpec((1, nc * rows_out, hbw), lambda i, h, j: (i, j, h)),
            pl.BlockSpec((1, hb, HEAD_DIM, HEAD_DIM), lambda i, h, j: (i, h, 0, 0)),
        ),
        scratch_shapes=[pltpu.VMEM((hb, HEAD_DIM, HEAD_DIM), F32)],
        compiler_params=_params("parallel", "parallel", "arbitrary"),
        name="delta_scan",
    )(u, wq, qkk, cd, proj, gnorm.reshape(1, HEAD_DIM), s0)


def _shift_rows(x, s, fill):
    rows = lax.broadcasted_iota(jnp.int32, x.shape, 0)
    return jnp.where(rows >= s, pltpu.roll(x, s, axis=0), fill)


def _lru_kernel(x_ref, gate_ref, st_ref, h0_ref, cw_ref, cb_ref, wr_ref, wi_ref, br_ref, bi_ref, lam_ref,
                y_ref, nb_ref, hn_ref, xp_ref, hc_ref, *, tt, first, blk):
    t = pl.program_id(2)
    nt = pl.num_programs(2)
    hl = cw_ref.shape[0] - 1

    @pl.when(t == 0)
    def _():
        xp_ref[HALO - hl:HALO, :] = st_ref[0]
        hc_ref[...] = h0_ref[0]

    @pl.when(t > 0)
    def _():
        xp_ref[HALO - hl:HALO, :] = xp_ref[HALO + tt - hl:HALO + tt, :]

    xp_ref[HALO:HALO + tt, :] = x_ref[0]
    xc = _conv_from_scratch(xp_ref, cw_ref, tt) + cb_ref[...]
    r_parts, i_parts = [], []
    for n in range(xc.shape[1] // blk):
        xn = xc[:, n * blk:(n + 1) * blk].astype(BF16)
        r_parts.append(jnp.dot(xn, wr_ref[n], preferred_element_type=F32))
        i_parts.append(jnp.dot(xn, wi_ref[n], preferred_element_type=F32))
    r = jax.nn.sigmoid(jnp.concatenate(r_parts, axis=1) + br_ref[...])
    gi = jax.nn.sigmoid(jnp.concatenate(i_parts, axis=1) + bi_ref[...])
    log_a = -C_LRU * r * _softplus(-lam_ref[...])
    a = jnp.exp(log_a)
    th = jnp.tanh(log_a)
    mult = jnp.sqrt(-2.0 * th / (1.0 - th))
    if first:
        reset = (lax.broadcasted_iota(jnp.int32, a.shape, 0) == 0) & (t == 0)
        a = jnp.where(reset, 0.0, a)
        mult = jnp.where(reset, 1.0, mult)
    u = mult * gi * xc

    s = 1
    while s < tt:
        u = a * _shift_rows(u, s, 0.0) + u
        a = a * _shift_rows(a, s, 1.0)
        s *= 2
    h = u + a * hc_ref[...]
    hc_ref[...] = h[tt - 1:tt, :]
    y_ref[0] = (h * jax.nn.gelu(gate_ref[0])).astype(y_ref.dtype)

    @pl.when(t == nt - 1)
    def _():
        nb_ref[0] = xp_ref[HALO + tt - hl:HALO + tt, :]
        hn_ref[0] = h[tt - 1:tt, :]


def _lru(proj, x_off, gate_off, st_conv, h0, cw, cb, wr, wi, br, bi, lam, *, first):
    b, t, _ = proj.shape
    nblk, blk, _ = wr.shape
    w = nblk * blk
    tt = _tile(t, 256)
    tc = _tile(w, 512)
    assert x_off % tc == 0 and gate_off % tc == 0 and tc % blk == 0
    kw = cw.shape[0]
    col = lambda a: a.reshape(1, w)
    rowspec = pl.BlockSpec((1, tc), lambda i, c, j: (0, c))
    return pl.pallas_call(
        functools.partial(_lru_kernel, tt=tt, first=first, blk=blk),
        out_shape=(
            jax.ShapeDtypeStruct((b, t, w), BF16),
            jax.ShapeDtypeStruct((b, kw - 1, w), F32),
            jax.ShapeDtypeStruct((b, 1, w), F32),
        ),
        grid=(b, w // tc, t // tt),
        in_specs=[
            pl.BlockSpec((1, tt, tc), lambda i, c, j: (i, j, x_off // tc + c)),
            pl.BlockSpec((1, tt, tc), lambda i, c, j: (i, j, gate_off // tc + c)),
            pl.BlockSpec((1, kw - 1, tc), lambda i, c, j: (i, 0, c)),
            pl.BlockSpec((1, 1, tc), lambda i, c, j: (i, 0, c)),
            pl.BlockSpec((kw, tc), lambda i, c, j: (0, c)),
            rowspec,
            pl.BlockSpec((tc // blk, blk, blk), lambda i, c, j: (c, 0, 0)),
            pl.BlockSpec((tc // blk, blk, blk), lambda i, c, j: (c, 0, 0)),
            rowspec, rowspec, rowspec,
        ],
        out_specs=(
            pl.BlockSpec((1, tt, tc), lambda i, c, j: (i, j, c)),
            pl.BlockSpec((1, kw - 1, tc), lambda i, c, j: (i, 0, c)),
            pl.BlockSpec((1, 1, tc), lambda i, c, j: (i, 0, c)),
        ),
        scratch_shapes=[pltpu.VMEM((HALO + tt, tc), F32), pltpu.VMEM((1, tc), F32)],
        compiler_params=_params("parallel", "parallel", "arbitrary"),
        name="rg_lru",
    )(proj, proj, st_conv, h0.reshape(b, 1, w), cw, col(cb), wr, wi, col(br), col(bi), col(lam))


def _ffn_up_kernel(h_ref, wg_ref, wv_ref, st_ref, cw_ref, cb_ref, act_ref, nb_ref, ug_ref,
                   *, seg, nseg, tiles_per_batch):
    i = pl.program_id(1)
    hl = cw_ref.shape[0] - 1
    h = h_ref[...]
    ug = jnp.dot(h, wg_ref[...], preferred_element_type=F32)
    uv = jnp.dot(h, wv_ref[...], preferred_element_type=F32)
    stride = HALO + seg
    pos = i % tiles_per_batch
    for s in range(nseg):
        base = s * stride
        if tiles_per_batch == 1:
            ug_ref[base + HALO - hl:base + HALO, :] = st_ref[s]
        else:
            @pl.when(pos == 0)
            def _():
                ug_ref[HALO - hl:HALO, :] = st_ref[0]

            @pl.when(pos > 0)
            def _():
                ug_ref[HALO - hl:HALO, :] = ug_ref[HALO + seg - hl:HALO + seg, :]

        ug_ref[base + HALO:base + HALO + seg, :] = ug[s * seg:(s + 1) * seg]
        y = _conv_from_scratch(ug_ref, cw_ref, seg, base) + cb_ref[...]
        act_ref[s * seg:(s + 1) * seg, :] = (jax.nn.gelu(y) * uv[s * seg:(s + 1) * seg]).astype(act_ref.dtype)
        if tiles_per_batch == 1:
            nb_ref[s] = ug_ref[base + HALO + seg - hl:base + HALO + seg, :]
        else:
            @pl.when(pos == tiles_per_batch - 1)
            def _():
                nb_ref[0] = ug_ref[HALO + seg - hl:HALO + seg, :]


def _ffn_up(h, w_up, state, cw, cb, *, t):
    m, d = h.shape
    f = w_up.shape[1] // 2
    b = m // t
    kw = cw.shape[0]
    tn = _tile(f, 512)
    if t >= 512:
        tm, nseg = _tile(t, 512), 1
    else:
        tm = _tile(m, max(t, 128))
        nseg = tm // t
    seg = tm // nseg
    tiles_per_batch = max(t // tm, 1)
    return pl.pallas_call(
        functools.partial(_ffn_up_kernel, seg=seg, nseg=nseg, tiles_per_batch=tiles_per_batch),
        out_shape=(jax.ShapeDtypeStruct((m, f), BF16), jax.ShapeDtypeStruct((b, kw - 1, f), F32)),
        grid=(f // tn, m // tm),
        in_specs=[
            pl.BlockSpec((tm, d), lambda j, i: (i, 0)),
            pl.BlockSpec((d, tn), lambda j, i: (0, j)),
            pl.BlockSpec((d, tn), lambda j, i: (0, f // tn + j)),
            pl.BlockSpec((nseg, kw - 1, tn), lambda j, i: ((i * tm) // (nseg * t), 0, j)),
            pl.BlockSpec((kw, tn), lambda j, i: (0, j)),
            pl.BlockSpec((1, tn), lambda j, i: (0, j)),
        ],
        out_specs=(
            pl.BlockSpec((tm, tn), lambda j, i: (i, j)),
            pl.BlockSpec((nseg, kw - 1, tn), lambda j, i: ((i * tm) // (nseg * t), 0, j)),
        ),
        scratch_shapes=[pltpu.VMEM((nseg * (HALO + seg), tn), F32)],
        compiler_params=_params("parallel", "arbitrary"),
        name="ffn_up",
    )(h, w_up, w_up, state, cw, cb.reshape(1, f))


def _pad_rows(a, rows):
    return a if a.shape[1] == rows else jnp.pad(a, ((0, 0), (0, rows - a.shape[1]), (0, 0)))


def _trunk(x, mod_all, st_delta, st_qkv, st_lru, st_lruconv, st_ffn, first, wts):
    b, t, d = x.shape
    m = b * t
    n_heads = wts["a_log"].shape[1]
    key_a = n_heads * HEAD_DIM
    qkv_a = 3 * key_a
    w_lru = wts["lru_lambda"].shape[1]
    depth = wts["a_log"].shape[0]
    z_off = qkv_a
    xb_off = z_off + key_a
    gb_off = xb_off + w_lru
    ga_off = gb_off + w_lru
    gbl_off = ga_off + d
    t_pad = -(-t // DELTA_CHUNK) * DELTA_CHUNK
    tm_big = 1024
    outs = {k: [] for k in ("delta", "qkv", "lru", "lruconv", "ffn")}

    for l in range(depth):
        sh1, sc1, gt1, sh2, sc2, gt2 = [c.reshape(b, 1, d) for c in jnp.split(mod_all[l], 6, axis=-1)]
        h = _rms_norm(x, wts["norm1_g"][l], sc1, sh1, out_dtype=BF16).reshape(m, d)
        proj = _matmul(h, wts["w_main"][l], out_dtype=F32, tm=tm_big, tn=1024, tk=d, name="in_proj")
        ab = _matmul(h, wts["w_ab"][l], out_dtype=F32, tm=tm_big, tn=LANE, tk=d, name="in_proj_gates")
        proj3 = proj.reshape(b, t, -1)

        qkv_c, new_qkv = _qkv_conv(proj3, st_qkv[l], wts["conv_qkv_w"][l], key_a=key_a, qkv_a=qkv_a)
        u, wq, qkk, cd = _delta_prep(_pad_rows(qkv_c, t_pad), _pad_rows(ab.reshape(b, t, LANE), t_pad),
                                     wts["a_log"][l], wts["dt_bias"][l], n_heads=n_heads, t_valid=t)
        o_a, new_delta = _delta_scan(u, wq, qkk, cd, proj3, z_off, wts["gnorm_a"][l], st_delta[l], t_real=t)
        y_a = _matmul(o_a.reshape(m, key_a), wts["w_ao"][l], out_dtype=F32, tm=tm_big, tn=1024, tk=key_a,
                      name="proj_a")

        y_lru, new_lruconv, new_lru = _lru(
            proj3, xb_off, gb_off, st_lruconv[l], st_lru[l], wts["conv_lru_w"][l], wts["conv_lru_b"][l],
            wts["lru_wr"][l], wts["lru_wi"][l], wts["lru_br"][l], wts["lru_bi"][l], wts["lru_lambda"][l],
            first=first)

        tm, tn = _tile(m, 512), _tile(d, 1024)
        assert ga_off % tn == 0 and gbl_off % tn == 0
        mixed = _matmul(
            y_lru.reshape(m, w_lru), wts["w_bo"][l], out_dtype=BF16, tm=tm, tn=tn, tk=w_lru,
            epilogue=_merge_epilogue,
            extras=[(y_a, (tm, tn), lambda i, j: (i, j)),
                    (proj, (tm, tn), lambda i, j: (i, ga_off // tn + j)),
                    (proj, (tm, tn), lambda i, j: (i, gbl_off // tn + j))],
            name="proj_b_merge")
        x2 = x.reshape(m, d)
        x2 = _matmul(mixed, wts["w_o"][l], out_dtype=F32, tm=tm, tn=tn, tk=d,
                     epilogue=functools.partial(_residual_epilogue_multi, t=t),
                     extras=[(x2, (tm, tn), lambda i, j: (i, j)), _gate_extra(gt1, t, tm, tn)], name="proj_o")

        h = _rms_norm(x2.reshape(b, t, d), wts["norm2_g"][l], sc2, sh2, out_dtype=BF16).reshape(m, d)
        act, new_ffn = _ffn_up(h, wts["w_up"][l], st_ffn[l], wts["conv_ffn_w"][l], wts["conv_ffn_b"][l], t=t)
        tm = _tile(m, 1024)
        x2 = _matmul(act, wts["w_down"][l], out_dtype=F32, tm=tm, tn=tn, tk=2048,
                     epilogue=functools.partial(_residual_epilogue_multi, t=t),
                     extras=[(x2, (tm, tn), lambda i, j: (i, j)), _gate_extra(gt2, t, tm, tn)], name="ffn_down")
        x = x2.reshape(b, t, d)

        outs["delta"].append(new_delta)
        outs["qkv"].append(new_qkv)
        outs["lru"].append(new_lru.reshape(b, w_lru))
        outs["lruconv"].append(new_lruconv)
        outs["ffn"].append(new_ffn)

    y = _rms_norm(x, wts["final_g"], out_dtype=F32)
    return (y,) + tuple(jnp.stack(outs[k]) for k in ("delta", "qkv", "lru", "lruconv", "ffn"))


def kernel(x_prompt, x_sample, state_delta, state_qkv_conv, state_lru, state_lru_conv, state_ffn_conv,
           c_prompt, c_sample, norm1_g, w_ada, b_ada, w_in, conv_qkv_w, a_log, dt_bias, gnorm_a, w_ao,
           conv_lru_w, conv_lru_b, lru_wr, lru_br, lru_wi, lru_bi, lru_lambda, w_bo, w_o, norm2_g, w_up,
           conv_ffn_w, conv_ffn_b, w_down, final_g):
    depth, n_heads = a_log.shape
    bp, bs = x_prompt.shape[0], x_sample.shape[0]
    qkv_a = 3 * n_heads * HEAD_DIM
    gate_cols = w_in[:, :, qkv_a:qkv_a + 2 * n_heads]
    wts = dict(
        w_main=jnp.concatenate([w_in[:, :, :qkv_a], w_in[:, :, qkv_a + 2 * n_heads:]], axis=-1).astype(BF16),
        w_ab=jnp.pad(gate_cols, ((0, 0), (0, 0), (0, LANE - 2 * n_heads))).astype(BF16),
        w_ao=w_ao.astype(BF16), w_bo=w_bo.astype(BF16), w_o=w_o.astype(BF16), w_up=w_up.astype(BF16),
        w_down=w_down.astype(BF16), lru_wr=lru_wr.astype(BF16), lru_wi=lru_wi.astype(BF16),
        norm1_g=norm1_g, conv_qkv_w=conv_qkv_w, a_log=a_log, dt_bias=dt_bias, gnorm_a=gnorm_a,
        conv_lru_w=conv_lru_w, conv_lru_b=conv_lru_b, lru_br=lru_br, lru_bi=lru_bi, lru_lambda=lru_lambda,
        norm2_g=norm2_g, conv_ffn_w=conv_ffn_w, conv_ffn_b=conv_ffn_b, final_g=final_g)

    rows = -(-(bp + bs) // 16) * 16
    c_all = jnp.pad(jnp.concatenate([c_prompt, c_sample], axis=0), ((0, rows - bp - bs), (0, 0)))
    mod_all = _ada_mod(c_all, w_ada, b_ada)

    dt = x_prompt.dtype
    zeros = lambda a: jnp.zeros((depth, bp) + a.shape[2:], a.dtype)
    prompt = _trunk(x_prompt, mod_all[:, :bp], zeros(state_delta).astype(F32), zeros(state_qkv_conv),
                    zeros(state_lru).astype(F32), zeros(state_lru_conv), zeros(state_ffn_conv), True, wts)
    sample = _trunk(x_sample, mod_all[:, bp:bp + bs], state_delta, state_qkv_conv, state_lru, state_lru_conv,
                    state_ffn_conv, False, wts)
    cast = lambda outs: tuple(o.astype(dt) for o in outs)
    return (prompt[0], sample[0]) + cast(prompt[1:]) + cast(sample[1:])
```

```python
import functools

import jax
import jax.numpy as jnp
from jax import lax
from jax.experimental import pallas as pl
from jax.experimental.pallas import tpu as pltpu

F32 = jnp.float32
BF16 = jnp.bfloat16

LANE = 128
SUBLANE = 8
VMEM_LIMIT_BYTES = 56 * 1024 * 1024

EPS = 1e-6
C_LRU = 8.0
HEAD_DIM = 128
DELTA_CHUNK = 128
HALO = SUBLANE


def _params(*sem):
    return pltpu.CompilerParams(dimension_semantics=sem, vmem_limit_bytes=VMEM_LIMIT_BYTES)


def _tile(dim, pref):
    if dim <= pref:
        return dim
    for t in range(pref - pref % LANE, 0, -LANE):
        if dim % t == 0:
            return t
    raise ValueError((dim, pref))


def _silu(x):
    return x * jax.nn.sigmoid(x)


def _softplus(x):
    return jnp.maximum(x, 0.0) + jnp.log1p(jnp.exp(-jnp.abs(x)))


def _bdot(a, b):
    return jnp.dot(a.astype(BF16), b.astype(BF16), preferred_element_type=F32)


def _ada_kernel(c_ref, w_ref, b_ref, o_ref):
    o_ref[0] = _bdot(_silu(c_ref[...]), w_ref[0]) + b_ref[0]


def _ada_mod(c_all, w_ada, b_ada):
    rows, d = c_all.shape
    depth, _, n = w_ada.shape
    tn = _tile(n, 512)
    return pl.pallas_call(
        _ada_kernel,
        out_shape=jax.ShapeDtypeStruct((depth, rows, n), F32),
        grid=(depth, n // tn),
        in_specs=[
            pl.BlockSpec((rows, d), lambda l, j: (0, 0)),
            pl.BlockSpec((1, d, tn), lambda l, j: (l, 0, j)),
            pl.BlockSpec((1, 1, tn), lambda l, j: (l, 0, j)),
        ],
        out_specs=pl.BlockSpec((1, rows, tn), lambda l, j: (l, 0, j)),
        compiler_params=_params("parallel", "parallel"),
        name="ada_mod",
    )(c_all, w_ada, b_ada.reshape(depth, 1, n))


def _norm_kernel(*refs, modulate):
    if modulate:
        x_ref, g_ref, sc_ref, sh_ref, o_ref = refs
    else:
        x_ref, g_ref, o_ref = refs
    x = x_ref[0]
    y = x * lax.rsqrt(jnp.mean(x * x, axis=-1, keepdims=True) + EPS) * g_ref[...]
    if modulate:
        y = y * (1.0 + sc_ref[0]) + sh_ref[0]
    o_ref[0] = y.astype(o_ref.dtype)


def _rms_norm(x, g, sc=None, sh=None, *, out_dtype):
    b, t, d = x.shape
    tt = _tile(t, 256)
    modulate = sc is not None
    row = pl.BlockSpec((1, 1, d), lambda i, j: (i, 0, 0))
    in_specs = [pl.BlockSpec((1, tt, d), lambda i, j: (i, j, 0)), pl.BlockSpec((1, d), lambda i, j: (0, 0))]
    args = [x, g.reshape(1, d)]
    if modulate:
        in_specs += [row, row]
        args += [sc, sh]
    return pl.pallas_call(
        functools.partial(_norm_kernel, modulate=modulate),
        out_shape=jax.ShapeDtypeStruct((b, t, d), out_dtype),
        grid=(b, t // tt),
        in_specs=in_specs,
        out_specs=pl.BlockSpec((1, tt, d), lambda i, j: (i, j, 0)),
        compiler_params=_params("parallel", "parallel"),
        name="rms_norm_mod" if modulate else "rms_norm",
    )(*args)


def _matmul_kernel(*refs, n_extra, nk, epilogue):
    a_ref, b_ref = refs[:2]
    extra = refs[2:2 + n_extra]
    o_ref = refs[2 + n_extra]

    def finish(acc):
        o_ref[...] = epilogue(acc, *[e[...] for e in extra]).astype(o_ref.dtype)

    part = jnp.dot(a_ref[...], b_ref[...], preferred_element_type=F32)
    if nk == 1:
        finish(part)
        return
    acc_ref = refs[3 + n_extra]
    k = pl.program_id(2)

    @pl.when(k == 0)
    def _():
        acc_ref[...] = part

    @pl.when(k > 0)
    def _():
        acc_ref[...] += part

    @pl.when(k == nk - 1)
    def _():
        finish(acc_ref[...])


def _matmul(a, b, layer, *, out_dtype, tm, tn, tk, epilogue=None, extras=(), n_outer=False, name):
    m, k = a.shape
    _, _, n = b.shape
    tm, tn, tk = _tile(m, tm), _tile(n, tn), _tile(k, tk)
    nk = k // tk
    if epilogue is None:
        epilogue = lambda acc: acc
    if n_outer:
        grid = (n // tn, m // tm, nk)
        order = lambda f: (lambda j, i, kk: f(i, j, kk))
    else:
        grid = (m // tm, n // tn, nk)
        order = lambda f: f
    in_specs = [pl.BlockSpec((tm, tk), order(lambda i, j, kk: (i, kk))),
                pl.BlockSpec((None, tk, tn), order(lambda i, j, kk: (layer, kk, j)))]
    for _, shape, imap in extras:
        in_specs.append(pl.BlockSpec(shape, order(functools.partial(lambda i, j, kk, f: f(i, j), f=imap))))
    return pl.pallas_call(
        functools.partial(_matmul_kernel, n_extra=len(extras), nk=nk, epilogue=epilogue),
        out_shape=jax.ShapeDtypeStruct((m, n), out_dtype),
        grid=grid,
        in_specs=in_specs,
        out_specs=pl.BlockSpec((tm, tn), order(lambda i, j, kk: (i, j))),
        scratch_shapes=[pltpu.VMEM((tm, tn), F32)] if nk > 1 else [],
        compiler_params=_params("parallel", "parallel", "arbitrary"),
        name=name,
    )(a, b, *[e[0] for e in extras])


def _mixer_merge_kernel(oa_ref, ob_ref, wa_ref, wb_ref, ga_ref, gb_ref, o_ref):
    y_a = jnp.dot(oa_ref[...], wa_ref[...], preferred_element_type=F32)
    y_b = jnp.dot(ob_ref[...], wb_ref[...], preferred_element_type=F32)
    o_ref[...] = (jax.nn.sigmoid(ga_ref[...]) * y_a + jax.nn.sigmoid(gb_ref[...]) * y_b).astype(o_ref.dtype)


def _mixer_merge(o_a, o_b, w_ao, w_bo, layer, proj, ga_off, gb_off):
    m, ka = o_a.shape
    _, kb = o_b.shape
    n = w_ao.shape[2]
    tm, tn = _tile(m, 512), _tile(n, 1024)
    assert ga_off % tn == 0 and gb_off % tn == 0
    return pl.pallas_call(
        _mixer_merge_kernel,
        out_shape=jax.ShapeDtypeStruct((m, n), BF16),
        grid=(m // tm, n // tn),
        in_specs=[
            pl.BlockSpec((tm, ka), lambda i, j: (i, 0)),
            pl.BlockSpec((tm, kb), lambda i, j: (i, 0)),
            pl.BlockSpec((None, ka, tn), lambda i, j: (layer, 0, j)),
            pl.BlockSpec((None, kb, tn), lambda i, j: (layer, 0, j)),
            pl.BlockSpec((tm, tn), lambda i, j: (i, ga_off // tn + j)),
            pl.BlockSpec((tm, tn), lambda i, j: (i, gb_off // tn + j)),
        ],
        out_specs=pl.BlockSpec((tm, tn), lambda i, j: (i, j)),
        compiler_params=_params("parallel", "parallel"),
        name="mixer_merge",
    )(o_a, o_b, w_ao, w_bo, proj, proj)


def _gate_extra(gate, t, tm, tn):
    if tm <= t:
        return (gate, (1, 1, tn), lambda i, j: ((i * tm) // t, 0, j))
    assert tm % t == 0
    return (gate, (tm // t, 1, tn), lambda i, j: (i, 0, j))


def _residual_epilogue_multi(acc, x, gate, *, t):
    nb = gate.shape[0]
    if nb == 1:
        return x + gate[0] * acc
    parts = [x[s * t:(s + 1) * t] + gate[s] * acc[s * t:(s + 1) * t] for s in range(nb)]
    return jnp.concatenate(parts, axis=0)


def _conv_from_scratch(xp_ref, w_ref, rows, base=0):
    kw = w_ref.shape[0]
    y = None
    for j in range(kw):
        term = w_ref[j:j + 1, :] * xp_ref[pl.ds(base + HALO - (kw - 1) + j, rows), :]
        y = term if y is None else y + term
    return y


def _qkv_conv_kernel(x_ref, st_ref, w_ref, o_ref, nb_ref, xp_ref, *, tt, n_qk_tiles, q_tiles):
    c, t = pl.program_id(1), pl.program_id(2)
    nt = pl.num_programs(2)
    hl = w_ref.shape[0] - 1

    @pl.when(t == 0)
    def _():
        xp_ref[HALO - hl:HALO, :] = st_ref[0]

    @pl.when(t > 0)
    def _():
        xp_ref[HALO - hl:HALO, :] = xp_ref[HALO + tt - hl:HALO + tt, :]

    xp_ref[HALO:HALO + tt, :] = x_ref[0]
    y = _silu(_conv_from_scratch(xp_ref, w_ref, tt))
    is_qk = c < n_qk_tiles
    scale = jnp.where(c < q_tiles, HEAD_DIM ** -0.5, 1.0).astype(F32)
    for hh in range(y.shape[1] // HEAD_DIM):
        blk = y[:, hh * HEAD_DIM:(hh + 1) * HEAD_DIM]
        inv = lax.rsqrt(jnp.sum(blk * blk, axis=-1, keepdims=True) + EPS) * scale
        o_ref[0, :, hh * HEAD_DIM:(hh + 1) * HEAD_DIM] = blk * jnp.where(is_qk, inv, 1.0)

    @pl.when(t == nt - 1)
    def _():
        nb_ref[0] = xp_ref[HALO + tt - hl:HALO + tt, :]


def _qkv_conv(proj, state, w, *, key_a, qkv_a):
    b, t, _ = proj.shape
    tt = _tile(t, 512)
    tc = _tile(key_a, 1024)
    kw = w.shape[0]
    return pl.pallas_call(
        functools.partial(_qkv_conv_kernel, tt=tt, n_qk_tiles=2 * key_a // tc, q_tiles=key_a // tc),
        out_shape=(jax.ShapeDtypeStruct((b, t, qkv_a), F32), jax.ShapeDtypeStruct((b, kw - 1, qkv_a), F32)),
        grid=(b, qkv_a // tc, t // tt),
        in_specs=[
            pl.BlockSpec((1, tt, tc), lambda i, c, j: (i, j, c)),
            pl.BlockSpec((1, kw - 1, tc), lambda i, c, j: (i, 0, c)),
            pl.BlockSpec((kw, tc), lambda i, c, j: (0, c)),
        ],
        out_specs=(
            pl.BlockSpec((1, tt, tc), lambda i, c, j: (i, j, c)),
            pl.BlockSpec((1, kw - 1, tc), lambda i, c, j: (i, 0, c)),
        ),
        scratch_shapes=[pltpu.VMEM((HALO + tt, tc), F32)],
        compiler_params=_params("parallel", "parallel", "arbitrary"),
        name="qkv_conv",
    )(proj, state, w)


def _block_diag(m):
    z = jnp.zeros((m.shape[0], HEAD_DIM), m.dtype)
    return jnp.concatenate([jnp.concatenate([m[:, :HEAD_DIM], z], axis=1),
                            jnp.concatenate([z, m[:, HEAD_DIM:]], axis=1)], axis=0)


def _delta_prep_kernel(q_ref, k_ref, v_ref, ab_ref, alog_ref, dtb_ref, lvl_ref, u_ref, wq_ref, qkk_ref, cd_ref,
                       t_ref, a_ref, rhs_ref, *, n_heads, t_valid):
    L, D, W = DELTA_CHUNK, HEAD_DIM, 2 * HEAD_DIM
    n_pairs = n_heads // 2
    n_levels = lvl_ref.shape[0]
    t = pl.program_id(1)
    ab = ab_ref[0]
    valid = (t * L + lax.broadcasted_iota(jnp.int32, ab.shape, 0)) < t_valid
    g = jnp.where(valid, -jnp.exp(alog_ref[...]) * _softplus(ab + dtb_ref[...]), 0.0)
    beta = jnp.where(valid, jax.nn.sigmoid(ab), 0.0)
    lane = lax.broadcasted_iota(jnp.int32, ab.shape, 1)
    tri = (lax.broadcasted_iota(jnp.int32, (L, L), 0) >= lax.broadcasted_iota(jnp.int32, (L, L), 1)).astype(F32)
    cum = jnp.dot(tri, g, precision=lax.Precision.HIGHEST, preferred_element_type=F32)
    cum_t = cum.T
    ii = lax.broadcasted_iota(jnp.int32, (L, W), 0)
    jj = lax.broadcasted_iota(jnp.int32, (L, W), 1) & (D - 1)
    causal = ii >= jj
    strict = ii > jj
    eye = jnp.where(ii == jj, 1.0, 0.0)
    second = lax.broadcasted_iota(jnp.int32, (1, W), 1) >= D

    def pair_cols(f):
        return jnp.where(second, f(1), f(0))

    for p in range(n_pairs):
        cols = slice(p * W, (p + 1) * W)
        pick = lambda x, idx: jnp.sum(jnp.where(lane == idx, x, 0.0), axis=1, keepdims=True)
        cum_col = pair_cols(lambda e: pick(cum, 2 * p + e))
        b_col = pair_cols(lambda e: pick(beta, n_heads + 2 * p + e))
        cum_row = jnp.concatenate([cum_t[2 * p:2 * p + 1, :], cum_t[2 * p + 1:2 * p + 2, :]], axis=1)
        g_last = cum_col[L - 1:L, :]
        gamma = jnp.where(causal, jnp.exp(jnp.where(causal, cum_col - cum_row, 0.0)), 0.0)
        e_col = jnp.exp(cum_col)

        q, k, v = q_ref[0, :, cols], k_ref[0, :, cols], v_ref[0, :, cols]
        kb = k * b_col
        kq = lax.dot_general(jnp.concatenate([kb, q], axis=0).astype(BF16), _block_diag(k.astype(BF16)),
                             (((1,), (1,)), ((), ())), preferred_element_type=F32)
        a_mat = jnp.where(strict, kq[:L] * gamma, 0.0)
        a_ref[p] = a_mat
        t_ref[p] = eye - a_mat * lvl_ref[0]
        vb, kbe = v * b_col, kb * e_col
        rhs_ref[p] = jnp.concatenate([vb[:, :D], kbe[:, :D], vb[:, D:], kbe[:, D:]], axis=1)
        wq_ref[0, L:2 * L, cols] = (q * e_col).astype(BF16)
        qkk_ref[0, 0:L, cols] = (kq[L:] * gamma).astype(BF16)
        kd = k * jnp.exp(g_last - cum_col)
        qkk_ref[0, L:2 * L, cols] = jnp.concatenate([kd[:, :D].T, kd[:, D:].T], axis=1).astype(BF16)
        cd_ref[0, 0, :, cols] = jnp.broadcast_to(jnp.exp(g_last), (SUBLANE, W))

    for s in range(1, n_levels):
        for p in range(n_pairs):
            tm = t_ref[p]
            tb = tm.astype(BF16)
            a_s = (a_ref[p] * lvl_ref[s]).astype(BF16)
            pm = jnp.dot(tb, _block_diag(a_s), preferred_element_type=F32)
            qm = jnp.dot(pm.astype(BF16), _block_diag(tb), preferred_element_type=F32)
            t_ref[p] = tm - qm

    for p in range(n_pairs):
        xb = (t_ref[p] - eye).astype(BF16)
        for e in range(2):
            hs = slice((2 * p + e) * D, (2 * p + e + 1) * D)
            rhs = rhs_ref[p, :, e * W:(e + 1) * W]
            sol = rhs + jnp.dot(xb[:, e * D:(e + 1) * D], rhs.astype(BF16), preferred_element_type=F32)
            u_ref[0, :, hs] = sol[:, :D]
            wq_ref[0, 0:L, hs] = sol[:, D:].astype(BF16)


def _level_masks():
    L = DELTA_CHUNK
    ii = jnp.arange(L)[:, None]
    jj = jnp.arange(2 * HEAD_DIM)[None, :] % HEAD_DIM
    return jnp.stack([(((ii >> s) ^ (jj >> s)) == 1) for s in range(L.bit_length() - 1)]).astype(F32)


def _delta_prep(qkv, ab, a_log, dt_bias, *, n_heads, t_valid):
    b, t, _ = qkv.shape
    L = DELTA_CHUNK
    assert L == HEAD_DIM and n_heads % 2 == 0
    hw = n_heads * HEAD_DIM
    nc = t // L
    lvl = _level_masks()
    pad = lambda a: jnp.pad(a.reshape(1, n_heads), ((0, 0), (0, LANE - n_heads)))
    tok = lambda off: pl.BlockSpec((1, L, hw), lambda i, j: (i, j, off))
    row = pl.BlockSpec((1, LANE), lambda i, j: (0, 0))
    pair_scratch = lambda width: pltpu.VMEM((n_heads // 2, L, width), F32)
    return pl.pallas_call(
        functools.partial(_delta_prep_kernel, n_heads=n_heads, t_valid=t_valid),
        out_shape=(
            jax.ShapeDtypeStruct((b, t, hw), F32),
            jax.ShapeDtypeStruct((b, 2 * t, hw), BF16),
            jax.ShapeDtypeStruct((b, 2 * t, hw), BF16),
            jax.ShapeDtypeStruct((b, nc, SUBLANE, hw), F32),
        ),
        grid=(b, nc),
        in_specs=[tok(0), tok(1), tok(2), pl.BlockSpec((1, L, LANE), lambda i, j: (i, j, 0)), row, row,
                  pl.BlockSpec(lvl.shape, lambda i, j: (0, 0, 0))],
        out_specs=(
            pl.BlockSpec((1, L, hw), lambda i, j: (i, j, 0)),
            pl.BlockSpec((1, 2 * L, hw), lambda i, j: (i, j, 0)),
            pl.BlockSpec((1, 2 * L, hw), lambda i, j: (i, j, 0)),
            pl.BlockSpec((1, 1, SUBLANE, hw), lambda i, j: (i, j, 0, 0)),
        ),
        scratch_shapes=[pair_scratch(2 * HEAD_DIM), pair_scratch(2 * HEAD_DIM), pair_scratch(4 * HEAD_DIM)],
        compiler_params=_params("parallel", "parallel"),
        name="delta_prep",
    )(qkv, qkv, qkv, ab, pad(a_log), pad(dt_bias), lvl)


def _delta_scan_kernel(u_ref, wq_ref, qkk_ref, cd_ref, z_ref, gn_ref, s0_ref, o_ref, so_ref, s_ref,
                       *, hb, nc, rows_out):
    L = DELTA_CHUNK
    t = pl.program_id(2)

    @pl.when(t == 0)
    def _():
        s_ref[...] = s0_ref[0]

    for c in range(nc):
        for hh in range(hb):
            hs = slice(hh * HEAD_DIM, (hh + 1) * HEAD_DIM)
            s = s_ref[hh]
            x1 = jnp.dot(wq_ref[0, 2 * c * L:2 * (c + 1) * L, hs], s.astype(BF16), preferred_element_type=F32)
            v_new = u_ref[0, c * L:(c + 1) * L, hs] - x1[:L]
            x2 = jnp.dot(qkk_ref[0, 2 * c * L:2 * (c + 1) * L, hs], v_new.astype(BF16),
                         preferred_element_type=F32)
            s_ref[hh] = s * cd_ref[0, c, 0:1, hs] + x2[L:]
            o = (x1[L:] + x2[:L])[:rows_out]
            z = z_ref[0, c * rows_out:(c + 1) * rows_out, hs]
            o = o * lax.rsqrt(jnp.mean(o * o, axis=-1, keepdims=True) + EPS) * gn_ref[...] * _silu(z)
            o_ref[0, c * rows_out:(c + 1) * rows_out, hs] = o.astype(o_ref.dtype)

    @pl.when(t == pl.num_programs(2) - 1)
    def _():
        so_ref[0] = s_ref[...]


def _delta_scan(u, wq, qkk, cd, proj, z_off, gnorm, s0, *, t_real):
    b, t, hw = u.shape
    L = DELTA_CHUNK
    n_heads = hw // HEAD_DIM
    hb = min(n_heads, 4)
    hbw = hb * HEAD_DIM
    assert z_off % hbw == 0
    nc = min(t // L, 4)
    rows_out = L if t_real == t else t_real
    assert t_real == t or t == L
    nt = t // (nc * L)
    return pl.pallas_call(
        functools.partial(_delta_scan_kernel, hb=hb, nc=nc, rows_out=rows_out),
        out_shape=(jax.ShapeDtypeStruct((b, t_real, hw), BF16), jax.ShapeDtypeStruct(s0.shape, F32)),
        grid=(b, n_heads // hb, nt),
        in_specs=[
            pl.BlockSpec((1, nc * L, hbw), lambda i, h, j: (i, j, h)),
            pl.BlockSpec((1, 2 * nc * L, hbw), lambda i, h, j: (i, j, h)),
            pl.BlockSpec((1, 2 * nc * L, hbw), lambda i, h, j: (i, j, h)),
            pl.BlockSpec((1, nc, SUBLANE, hbw), lambda i, h, j: (i, j, 0, h)),
            pl.BlockSpec((1, nc * rows_out, hbw), lambda i, h, j: (i, j, z_off // hbw + h)),
            pl.BlockSpec((1, HEAD_DIM), lambda i, h, j: (0, 0)),
            pl.BlockSpec((1, hb, HEAD_DIM, HEAD_DIM), lambda i, h, j: (i, h, 0, 0)),
        ],
        out_specs=(
            pl.BlockSpec((1, nc * rows_out, hbw), lambda i, h, j: (i, j, h)),
            pl.BlockSpec((1, hb, HEAD_DIM, HEAD_DIM), lambda i, h, j: (i, h, 0, 0)),
        ),
        scratch_shapes=[pltpu.VMEM((hb, HEAD_DIM, HEAD_DIM), F32)],
        compiler_params=_params("parallel", "parallel", "arbitrary"),
        name="delta_scan",
    )(u, wq, qkk, cd, proj, gnorm.reshape(1, HEAD_DIM), s0)


def _lru_kernel(x_ref, gate_ref, st_ref, h0_ref, cw_ref, cb_ref, wr_ref, wi_ref, br_ref, bi_ref, lam_ref,
                y_ref, nb_ref, hn_ref, xp_ref, hc_ref, *, tt, first, blk):
    t = pl.program_id(2)
    nt = pl.num_programs(2)
    hl = cw_ref.shape[0] - 1

    @pl.when(t == 0)
    def _():
        xp_ref[HALO - hl:HALO, :] = st_ref[0]
        hc_ref[...] = h0_ref[0]

    @pl.when(t > 0)
    def _():
        xp_ref[HALO - hl:HALO, :] = xp_ref[HALO + tt - hl:HALO + tt, :]

    xp_ref[HALO:HALO + tt, :] = x_ref[0]
    xc = _conv_from_scratch(xp_ref, cw_ref, tt) + cb_ref[...]
    r_parts, i_parts = [], []
    for n in range(xc.shape[1] // blk):
        xn = xc[:, n * blk:(n + 1) * blk].astype(BF16)
        r_parts.append(jnp.dot(xn, wr_ref[n], preferred_element_type=F32))
        i_parts.append(jnp.dot(xn, wi_ref[n], preferred_element_type=F32))
    r = jax.nn.sigmoid(jnp.concatenate(r_parts, axis=1) + br_ref[...])
    gi = jax.nn.sigmoid(jnp.concatenate(i_parts, axis=1) + bi_ref[...])
    log_a = -C_LRU * r * _softplus(-lam_ref[...])
    a = jnp.exp(log_a)
    th = jnp.tanh(log_a)
    mult = jnp.sqrt(-2.0 * th / (1.0 - th))
    if first:
        reset = (lax.broadcasted_iota(jnp.int32, a.shape, 0) == 0) & (t == 0)
        a = jnp.where(reset, 0.0, a)
        mult = jnp.where(reset, 1.0, mult)
    u = mult * gi * xc

    in_group = lax.broadcasted_iota(jnp.int32, a.shape, 0) & (SUBLANE - 1)
    s = 1
    while s < SUBLANE:
        keep = in_group >= s
        u = a * jnp.where(keep, pltpu.roll(u, s, axis=0), 0.0) + u
        a = a * jnp.where(keep, pltpu.roll(a, s, axis=0), 1.0)
        s *= 2
    carry = hc_ref[...]
    groups = []
    for r in range(tt // SUBLANE):
        rows = slice(r * SUBLANE, (r + 1) * SUBLANE)
        groups.append(u[rows] + a[rows] * carry)
        carry = groups[-1][SUBLANE - 1:SUBLANE, :]
    h = jnp.concatenate(groups, axis=0)
    hc_ref[...] = carry
    y_ref[0] = (h * jax.nn.gelu(gate_ref[0])).astype(y_ref.dtype)

    @pl.when(t == nt - 1)
    def _():
        nb_ref[0] = xp_ref[HALO + tt - hl:HALO + tt, :]
        hn_ref[0] = h[tt - 1:tt, :]


def _lru(proj, x_off, gate_off, st_conv, h0, cw, cb, wr, wi, layer, br, bi, lam, *, first):
    b, t, _ = proj.shape
    _, nblk, blk, _ = wr.shape
    w = nblk * blk
    tt = _tile(t, 256)
    tc = _tile(w, 512)
    assert x_off % tc == 0 and gate_off % tc == 0 and tc % blk == 0
    kw = cw.shape[0]
    col = lambda a: a.reshape(1, w)
    rowspec = pl.BlockSpec((1, tc), lambda i, c, j: (0, c))
    return pl.pallas_call(
        functools.partial(_lru_kernel, tt=tt, first=first, blk=blk),
        out_shape=(
            jax.ShapeDtypeStruct((b, t, w), BF16),
            jax.ShapeDtypeStruct((b, kw - 1, w), F32),
            jax.ShapeDtypeStruct((b, 1, w), F32),
        ),
        grid=(b, w // tc, t // tt),
        in_specs=[
            pl.BlockSpec((1, tt, tc), lambda i, c, j: (i, j, x_off // tc + c)),
            pl.BlockSpec((1, tt, tc), lambda i, c, j: (i, j, gate_off // tc + c)),
            pl.BlockSpec((1, kw - 1, tc), lambda i, c, j: (i, 0, c)),
            pl.BlockSpec((1, 1, tc), lambda i, c, j: (i, 0, c)),
            pl.BlockSpec((kw, tc), lambda i, c, j: (0, c)),
            rowspec,
            pl.BlockSpec((None, tc // blk, blk, blk), lambda i, c, j: (layer, c, 0, 0)),
            pl.BlockSpec((None, tc // blk, blk, blk), lambda i, c, j: (layer, c, 0, 0)),
            rowspec, rowspec, rowspec,
        ],
        out_specs=(
            pl.BlockSpec((1, tt, tc), lambda i, c, j: (i, j, c)),
            pl.BlockSpec((1, kw - 1, tc), lambda i, c, j: (i, 0, c)),
            pl.BlockSpec((1, 1, tc), lambda i, c, j: (i, 0, c)),
        ),
        scratch_shapes=[pltpu.VMEM((HALO + tt, tc), F32), pltpu.VMEM((1, tc), F32)],
        compiler_params=_params("parallel", "parallel", "arbitrary"),
        name="rg_lru",
    )(proj, proj, st_conv, h0.reshape(b, 1, w), cw, col(cb), wr, wi, col(br), col(bi), col(lam))


def _ffn_up_kernel(h_ref, wg_ref, wv_ref, st_ref, cw_ref, cb_ref, act_ref, nb_ref, ug_ref,
                   *, seg, nseg, tiles_per_batch, sub):
    i = pl.program_id(1)
    hl = cw_ref.shape[0] - 1
    stride = HALO + seg
    pos = i % tiles_per_batch
    if nseg == 1:
        if tiles_per_batch == 1:
            ug_ref[HALO - hl:HALO, :] = st_ref[0]
        else:
            @pl.when(pos == 0)
            def _():
                ug_ref[HALO - hl:HALO, :] = st_ref[0]

            @pl.when(pos > 0)
            def _():
                ug_ref[HALO - hl:HALO, :] = ug_ref[HALO + seg - hl:HALO + seg, :]

        for r in range(seg // sub):
            rows = slice(r * sub, (r + 1) * sub)
            h = h_ref[rows, :]
            ug = jnp.dot(h, wg_ref[...], preferred_element_type=F32)
            uv = jnp.dot(h, wv_ref[...], preferred_element_type=F32)
            ug_ref[HALO + r * sub:HALO + (r + 1) * sub, :] = ug
            y = _conv_from_scratch(ug_ref, cw_ref, sub, r * sub) + cb_ref[...]
            act_ref[rows, :] = (jax.nn.gelu(y) * uv).astype(act_ref.dtype)

        if tiles_per_batch == 1:
            nb_ref[0] = ug_ref[HALO + seg - hl:HALO + seg, :]
        else:
            @pl.when(pos == tiles_per_batch - 1)
            def _():
                nb_ref[0] = ug_ref[HALO + seg - hl:HALO + seg, :]
        return

    h = h_ref[...]
    ug = jnp.dot(h, wg_ref[...], preferred_element_type=F32)
    uv = jnp.dot(h, wv_ref[...], preferred_element_type=F32)
    for s in range(nseg):
        base = s * stride
        ug_ref[base + HALO - hl:base + HALO, :] = st_ref[s]
        ug_ref[base + HALO:base + HALO + seg, :] = ug[s * seg:(s + 1) * seg]
        y = _conv_from_scratch(ug_ref, cw_ref, seg, base) + cb_ref[...]
        act_ref[s * seg:(s + 1) * seg, :] = (jax.nn.gelu(y) * uv[s * seg:(s + 1) * seg]).astype(act_ref.dtype)
        nb_ref[s] = ug_ref[base + HALO + seg - hl:base + HALO + seg, :]


def _ffn_up(h, w_up, layer, state, cw, cb, *, t):
    m, d = h.shape
    f = w_up.shape[2] // 2
    b = m // t
    kw = cw.shape[0]
    tn = _tile(f, 512)
    if t >= 512:
        tm, nseg = _tile(t, 1024), 1
    else:
        tm = _tile(m, max(t, 128))
        nseg = tm // t
    seg = tm // nseg
    sub = _tile(seg, 256)
    tiles_per_batch = max(t // tm, 1)
    assert nseg == 1 or tiles_per_batch == 1
    return pl.pallas_call(
        functools.partial(_ffn_up_kernel, seg=seg, nseg=nseg, tiles_per_batch=tiles_per_batch, sub=sub),
        out_shape=(jax.ShapeDtypeStruct((m, f), BF16), jax.ShapeDtypeStruct((b, kw - 1, f), F32)),
        grid=(f // tn, m // tm),
        in_specs=[
            pl.BlockSpec((tm, d), lambda j, i: (i, 0)),
            pl.BlockSpec((None, d, tn), lambda j, i: (layer, 0, j)),
            pl.BlockSpec((None, d, tn), lambda j, i: (layer, 0, f // tn + j)),
            pl.BlockSpec((nseg, kw - 1, tn), lambda j, i: ((i * tm) // (nseg * t), 0, j)),
            pl.BlockSpec((kw, tn), lambda j, i: (0, j)),
            pl.BlockSpec((1, tn), lambda j, i: (0, j)),
        ],
        out_specs=(
            pl.BlockSpec((tm, tn), lambda j, i: (i, j)),
            pl.BlockSpec((nseg, kw - 1, tn), lambda j, i: ((i * tm) // (nseg * t), 0, j)),
        ),
        scratch_shapes=[pltpu.VMEM((nseg * (HALO + seg), tn), F32)],
        compiler_params=_params("parallel", "arbitrary"),
        name="ffn_up",
    )(h, w_up, w_up, state, cw, cb.reshape(1, f))


def _pad_rows(a, rows):
    return a if a.shape[1] == rows else jnp.pad(a, ((0, 0), (0, rows - a.shape[1]), (0, 0)))


def _trunk(x, mod_all, st_delta, st_qkv, st_lru, st_lruconv, st_ffn, first, wts):
    b, t, d = x.shape
    m = b * t
    n_heads = wts["a_log"].shape[1]
    key_a = n_heads * HEAD_DIM
    qkv_a = 3 * key_a
    w_lru = wts["lru_lambda"].shape[1]
    depth = wts["a_log"].shape[0]
    z_off = qkv_a
    xb_off = z_off + key_a
    gb_off = xb_off + w_lru
    ga_off = gb_off + w_lru
    gbl_off = ga_off + d
    t_pad = -(-t // DELTA_CHUNK) * DELTA_CHUNK
    tm_big = 1024
    outs = {k: [] for k in ("delta", "qkv", "lru", "lruconv", "ffn")}

    for l in range(depth):
        sh1, sc1, gt1, sh2, sc2, gt2 = [c.reshape(b, 1, d) for c in jnp.split(mod_all[l], 6, axis=-1)]
        h = _rms_norm(x, wts["norm1_g"][l], sc1, sh1, out_dtype=BF16).reshape(m, d)
        proj = _matmul(h, wts["w_main"], l, out_dtype=F32, tm=tm_big, tn=1024, tk=d, name="in_proj")
        ab = _matmul(h, wts["w_ab"], l, out_dtype=F32, tm=tm_big, tn=LANE, tk=d, name="in_proj_gates")
        proj3 = proj.reshape(b, t, -1)

        qkv_c, new_qkv = _qkv_conv(proj3, st_qkv[l], wts["conv_qkv_w"][l], key_a=key_a, qkv_a=qkv_a)
        u, wq, qkk, cd = _delta_prep(_pad_rows(qkv_c, t_pad), _pad_rows(ab.reshape(b, t, LANE), t_pad),
                                     wts["a_log"][l], wts["dt_bias"][l], n_heads=n_heads, t_valid=t)
        o_a, new_delta = _delta_scan(u, wq, qkk, cd, proj3, z_off, wts["gnorm_a"][l], st_delta[l], t_real=t)

        y_lru, new_lruconv, new_lru = _lru(
            proj3, xb_off, gb_off, st_lruconv[l], st_lru[l], wts["conv_lru_w"][l], wts["conv_lru_b"][l],
            wts["lru_wr"], wts["lru_wi"], l, wts["lru_br"][l], wts["lru_bi"][l], wts["lru_lambda"][l],
            first=first)

        mixed = _mixer_merge(o_a.reshape(m, key_a), y_lru.reshape(m, w_lru), wts["w_ao"], wts["w_bo"], l,
                             proj, ga_off, gbl_off)
        tm, tn = _tile(m, 512), _tile(d, 1024)
        x2 = x.reshape(m, d)
        x2 = _matmul(mixed, wts["w_o"], l, out_dtype=F32, tm=tm, tn=tn, tk=d,
                     epilogue=functools.partial(_residual_epilogue_multi, t=t),
                     extras=[(x2, (tm, tn), lambda i, j: (i, j)), _gate_extra(gt1, t, tm, tn)], name="proj_o")

        h = _rms_norm(x2.reshape(b, t, d), wts["norm2_g"][l], sc2, sh2, out_dtype=BF16).reshape(m, d)
        act, new_ffn = _ffn_up(h, wts["w_up"], l, st_ffn[l], wts["conv_ffn_w"][l], wts["conv_ffn_b"][l], t=t)
        tm, tn = _tile(m, 256), _tile(d, 512)
        x2 = _matmul(act, wts["w_down"], l, out_dtype=F32, tm=tm, tn=tn, tk=act.shape[1], n_outer=True,
                     epilogue=functools.partial(_residual_epilogue_multi, t=t),
                     extras=[(x2, (tm, tn), lambda i, j: (i, j)), _gate_extra(gt2, t, tm, tn)], name="ffn_down")
        x = x2.reshape(b, t, d)

        outs["delta"].append(new_delta)
        outs["qkv"].append(new_qkv)
        outs["lru"].append(new_lru.reshape(b, w_lru))
        outs["lruconv"].append(new_lruconv)
        outs["ffn"].append(new_ffn)

    y = _rms_norm(x, wts["final_g"], out_dtype=F32)
    return (y,) + tuple(jnp.stack(outs[k]) for k in ("delta", "qkv", "lru", "lruconv", "ffn"))


def kernel(x_prompt, x_sample, state_delta, state_qkv_conv, state_lru, state_lru_conv, state_ffn_conv,
           c_prompt, c_sample, norm1_g, w_ada, b_ada, w_in, conv_qkv_w, a_log, dt_bias, gnorm_a, w_ao,
           conv_lru_w, conv_lru_b, lru_wr, lru_br, lru_wi, lru_bi, lru_lambda, w_bo, w_o, norm2_g, w_up,
           conv_ffn_w, conv_ffn_b, w_down, final_g):
    depth, n_heads = a_log.shape
    bp, bs = x_prompt.shape[0], x_sample.shape[0]
    qkv_a = 3 * n_heads * HEAD_DIM
    gate_cols = w_in[:, :, qkv_a:qkv_a + 2 * n_heads]
    wts = dict(
        w_main=jnp.concatenate([w_in[:, :, :qkv_a], w_in[:, :, qkv_a + 2 * n_heads:]], axis=-1).astype(BF16),
        w_ab=jnp.pad(gate_cols, ((0, 0), (0, 0), (0, LANE - 2 * n_heads))).astype(BF16),
        w_ao=w_ao.astype(BF16), w_bo=w_bo.astype(BF16), w_o=w_o.astype(BF16), w_up=w_up.astype(BF16),
        w_down=w_down.astype(BF16), lru_wr=lru_wr.astype(BF16), lru_wi=lru_wi.astype(BF16),
        norm1_g=norm1_g, conv_qkv_w=conv_qkv_w, a_log=a_log, dt_bias=dt_bias, gnorm_a=gnorm_a,
        conv_lru_w=conv_lru_w, conv_lru_b=conv_lru_b, lru_br=lru_br, lru_bi=lru_bi, lru_lambda=lru_lambda,
        norm2_g=norm2_g, conv_ffn_w=conv_ffn_w, conv_ffn_b=conv_ffn_b, final_g=final_g)

    rows = -(-(bp + bs) // 16) * 16
    c_all = jnp.pad(jnp.concatenate([c_prompt, c_sample], axis=0), ((0, rows - bp - bs), (0, 0)))
    mod_all = _ada_mod(c_all, w_ada, b_ada)

    dt = x_prompt.dtype
    zeros = lambda a: jnp.zeros((depth, bp) + a.shape[2:], a.dtype)
    prompt = _trunk(x_prompt, mod_all[:, :bp], zeros(state_delta).astype(F32), zeros(state_qkv_conv),
                    zeros(state_lru).astype(F32), zeros(state_lru_conv), zeros(state_ffn_conv), True, wts)
    sample = _trunk(x_sample, mod_all[:, bp:bp + bs], state_delta, state_qkv_conv, state_lru, state_lru_conv,
                    state_ffn_conv, False, wts)
    cast = lambda outs: tuple(o.astype(dt) for o in outs)
    return (prompt[0], sample[0]) + cast(prompt[1:]) + cast(sample[1:])
```

```python
import functools
import math

import jax
import jax.numpy as jnp
from jax import lax
from jax.experimental import pallas as pl
from jax.experimental.pallas import tpu as pltpu

F32 = jnp.float32
BF16 = jnp.bfloat16

LANE = 128
SUBLANE = 8
VMEM_LIMIT_BYTES = 56 * 1024 * 1024

EPS = 1e-6
C_LRU = 8.0
HEAD_DIM = 128
DELTA_CHUNK = 128
HALO = SUBLANE


def _params(*sem):
    return pltpu.CompilerParams(dimension_semantics=sem, vmem_limit_bytes=VMEM_LIMIT_BYTES)


def _tile(dim, pref):
    if dim <= pref:
        return dim
    for t in range(pref - pref % LANE, 0, -LANE):
        if dim % t == 0:
            return t
    raise ValueError((dim, pref))


def _silu(x):
    return x * jax.nn.sigmoid(x)


def _softplus(x):
    return jnp.maximum(x, 0.0) + jnp.log1p(jnp.exp(-jnp.abs(x)))


def _bdot(a, b):
    return jnp.dot(a.astype(BF16), b.astype(BF16), preferred_element_type=F32)


def _ada_kernel(c_ref, w_ref, b_ref, o_ref):
    o_ref[0] = _bdot(_silu(c_ref[...]), w_ref[0]) + b_ref[0]


def _ada_mod(c_all, w_ada, b_ada):
    rows, d = c_all.shape
    depth, _, n = w_ada.shape
    tn = _tile(n, 512)
    return pl.pallas_call(
        _ada_kernel,
        out_shape=jax.ShapeDtypeStruct((depth, rows, n), F32),
        grid=(depth, n // tn),
        in_specs=[
            pl.BlockSpec((rows, d), lambda l, j: (0, 0)),
            pl.BlockSpec((1, d, tn), lambda l, j: (l, 0, j)),
            pl.BlockSpec((1, 1, tn), lambda l, j: (l, 0, j)),
        ],
        out_specs=pl.BlockSpec((1, rows, tn), lambda l, j: (l, 0, j)),
        compiler_params=_params("parallel", "parallel"),
        name="ada_mod",
    )(c_all, w_ada, b_ada.reshape(depth, 1, n))


def _norm_kernel(*refs, modulate):
    if modulate:
        x_ref, g_ref, sc_ref, sh_ref, o_ref = refs
    else:
        x_ref, g_ref, o_ref = refs
    x = x_ref[0]
    y = x * lax.rsqrt(jnp.mean(x * x, axis=-1, keepdims=True) + EPS) * g_ref[...]
    if modulate:
        y = y * (1.0 + sc_ref[0]) + sh_ref[0]
    o_ref[0] = y.astype(o_ref.dtype)


def _rms_norm(x, g, sc=None, sh=None, *, out_dtype):
    b, t, d = x.shape
    tt = _tile(t, 512)
    modulate = sc is not None
    row = pl.BlockSpec((1, 1, d), lambda i, j: (i, 0, 0))
    in_specs = [pl.BlockSpec((1, tt, d), lambda i, j: (i, j, 0)), pl.BlockSpec((1, d), lambda i, j: (0, 0))]
    args = [x, g.reshape(1, d)]
    if modulate:
        in_specs += [row, row]
        args += [sc, sh]
    return pl.pallas_call(
        functools.partial(_norm_kernel, modulate=modulate),
        out_shape=jax.ShapeDtypeStruct((b, t, d), out_dtype),
        grid=(b, t // tt),
        in_specs=in_specs,
        out_specs=pl.BlockSpec((1, tt, d), lambda i, j: (i, j, 0)),
        compiler_params=_params("parallel", "parallel"),
        name="rms_norm_mod" if modulate else "rms_norm",
    )(*args)


def _matmul_kernel(*refs, n_extra, nk, epilogue):
    a_ref, b_ref = refs[:2]
    extra = refs[2:2 + n_extra]
    o_ref = refs[2 + n_extra]

    def finish(acc):
        o_ref[...] = epilogue(acc, *[e[...] for e in extra]).astype(o_ref.dtype)

    part = jnp.dot(a_ref[...], b_ref[...], preferred_element_type=F32)
    if nk == 1:
        finish(part)
        return
    acc_ref = refs[3 + n_extra]
    k = pl.program_id(2)

    @pl.when(k == 0)
    def _():
        acc_ref[...] = part

    @pl.when(k > 0)
    def _():
        acc_ref[...] += part

    @pl.when(k == nk - 1)
    def _():
        finish(acc_ref[...])


def _matmul(a, b, layer, *, out_dtype, tm, tn, tk, epilogue=None, extras=(), n_outer=False, name):
    m, k = a.shape
    _, _, n = b.shape
    tm, tn, tk = _tile(m, tm), _tile(n, tn), _tile(k, tk)
    nk = k // tk
    if epilogue is None:
        epilogue = lambda acc: acc
    if n_outer:
        grid = (n // tn, m // tm, nk)
        order = lambda f: (lambda j, i, kk: f(i, j, kk))
    else:
        grid = (m // tm, n // tn, nk)
        order = lambda f: f
    in_specs = [pl.BlockSpec((tm, tk), order(lambda i, j, kk: (i, kk))),
                pl.BlockSpec((None, tk, tn), order(lambda i, j, kk: (layer, kk, j)))]
    for _, shape, imap in extras:
        in_specs.append(pl.BlockSpec(shape, order(functools.partial(lambda i, j, kk, f: f(i, j), f=imap))))
    return pl.pallas_call(
        functools.partial(_matmul_kernel, n_extra=len(extras), nk=nk, epilogue=epilogue),
        out_shape=jax.ShapeDtypeStruct((m, n), out_dtype),
        grid=grid,
        in_specs=in_specs,
        out_specs=pl.BlockSpec((tm, tn), order(lambda i, j, kk: (i, j))),
        scratch_shapes=[pltpu.VMEM((tm, tn), F32)] if nk > 1 else [],
        compiler_params=_params("parallel", "parallel", "arbitrary"),
        name=name,
    )(a, b, *[e[0] for e in extras])


def _mixer_merge_kernel(oa_ref, ob_ref, wa_ref, wb_ref, ga_ref, gb_ref, o_ref):
    y_a = jnp.dot(oa_ref[...], wa_ref[...], preferred_element_type=F32)
    y_b = jnp.dot(ob_ref[...], wb_ref[...], preferred_element_type=F32)
    o_ref[...] = (jax.nn.sigmoid(ga_ref[...]) * y_a + jax.nn.sigmoid(gb_ref[...]) * y_b).astype(o_ref.dtype)


def _mixer_merge(o_a, o_b, w_ao, w_bo, layer, proj, ga_off, gb_off):
    m, ka = o_a.shape
    _, kb = o_b.shape
    n = w_ao.shape[2]
    tm, tn = _tile(m, 512), _tile(math.gcd(n, ga_off, gb_off), 1024)
    return pl.pallas_call(
        _mixer_merge_kernel,
        out_shape=jax.ShapeDtypeStruct((m, n), BF16),
        grid=(n // tn, m // tm),
        in_specs=[
            pl.BlockSpec((tm, ka), lambda j, i: (i, 0)),
            pl.BlockSpec((tm, kb), lambda j, i: (i, 0)),
            pl.BlockSpec((None, ka, tn), lambda j, i: (layer, 0, j)),
            pl.BlockSpec((None, kb, tn), lambda j, i: (layer, 0, j)),
            pl.BlockSpec((tm, tn), lambda j, i: (i, ga_off // tn + j)),
            pl.BlockSpec((tm, tn), lambda j, i: (i, gb_off // tn + j)),
        ],
        out_specs=pl.BlockSpec((tm, tn), lambda j, i: (i, j)),
        compiler_params=_params("parallel", "parallel"),
        name="mixer_merge",
    )(o_a, o_b, w_ao, w_bo, proj, proj)


def _gate_extra(gate, t, tm, tn):
    if tm <= t:
        return (gate, (1, 1, tn), lambda i, j: ((i * tm) // t, 0, j))
    assert tm % t == 0
    return (gate, (tm // t, 1, tn), lambda i, j: (i, 0, j))


def _residual_epilogue_multi(acc, x, gate, *, t):
    nb = gate.shape[0]
    if nb == 1:
        return x + gate[0] * acc
    parts = [x[s * t:(s + 1) * t] + gate[s] * acc[s * t:(s + 1) * t] for s in range(nb)]
    return jnp.concatenate(parts, axis=0)


def _conv_from_scratch(xp_ref, w_ref, rows, base=0):
    kw = w_ref.shape[0]
    y = None
    for j in range(kw):
        term = w_ref[j:j + 1, :] * xp_ref[pl.ds(base + HALO - (kw - 1) + j, rows), :]
        y = term if y is None else y + term
    return y


def _conv_proj_rows(h_ref, st_ref, cw_ref, nb_ref, scr_ref, project, emit, *, seg, nseg, tiles_per_batch, sub):
    i = pl.program_id(1)
    hl = cw_ref.shape[0] - 1
    pos = i % tiles_per_batch
    if nseg == 1:
        if tiles_per_batch == 1:
            scr_ref[HALO - hl:HALO, :] = st_ref[0]
        else:
            @pl.when(pos == 0)
            def _():
                scr_ref[HALO - hl:HALO, :] = st_ref[0]

            @pl.when(pos > 0)
            def _():
                scr_ref[HALO - hl:HALO, :] = scr_ref[HALO + seg - hl:HALO + seg, :]

        def start(r):
            x, aux = project(h_ref[r * sub:(r + 1) * sub, :])
            scr_ref[HALO + r * sub:HALO + (r + 1) * sub, :] = x
            return aux

        n_sub = seg // sub
        aux = start(0)
        for r in range(n_sub):
            aux_next = start(r + 1) if r + 1 < n_sub else None
            emit(slice(r * sub, (r + 1) * sub), _conv_from_scratch(scr_ref, cw_ref, sub, r * sub), aux)
            aux = aux_next

        if tiles_per_batch == 1:
            nb_ref[0] = scr_ref[HALO + seg - hl:HALO + seg, :]
        else:
            @pl.when(pos == tiles_per_batch - 1)
            def _():
                nb_ref[0] = scr_ref[HALO + seg - hl:HALO + seg, :]
        return

    x, aux = project(h_ref[...])
    for s in range(nseg):
        base = s * (HALO + seg)
        rows = slice(s * seg, (s + 1) * seg)
        scr_ref[base + HALO - hl:base + HALO, :] = st_ref[s]
        scr_ref[base + HALO:base + HALO + seg, :] = x[rows]
        emit(rows, _conv_from_scratch(scr_ref, cw_ref, seg, base), None if aux is None else aux[rows])
        nb_ref[s] = scr_ref[base + HALO + seg - hl:base + HALO + seg, :]


def _conv_proj_tiling(m, t, tm_pref):
    if t >= 512:
        tm, nseg = _tile(t, tm_pref), 1
    else:
        tm = _tile(m, max(t, 128))
        nseg = tm // t
    seg = tm // nseg
    tiles_per_batch = max(t // tm, 1)
    assert nseg == 1 or tiles_per_batch == 1
    return tm, nseg, seg, _tile(seg, 256), tiles_per_batch


def _qkv_proj_kernel(h_ref, w_ref, st_ref, cw_ref, o_ref, nb_ref, scr_ref, *, n_qk_tiles, q_tiles, **tiling):
    j = pl.program_id(0)
    is_qk = j < n_qk_tiles
    scale = jnp.where(j < q_tiles, HEAD_DIM ** -0.5, 1.0).astype(F32)

    def project(h):
        return jnp.dot(h, w_ref[...], preferred_element_type=F32), None

    def emit(rows, conv, _):
        y = _silu(conv)
        for hh in range(y.shape[1] // HEAD_DIM):
            hs = slice(hh * HEAD_DIM, (hh + 1) * HEAD_DIM)
            blk = y[:, hs]
            inv = lax.rsqrt(jnp.sum(blk * blk, axis=-1, keepdims=True) + EPS) * scale
            o_ref[rows, hs] = blk * jnp.where(is_qk, inv, 1.0)

    _conv_proj_rows(h_ref, st_ref, cw_ref, nb_ref, scr_ref, project, emit, **tiling)


def _qkv_proj(h, w_qkv, layer, state, cw, *, t, key_a):
    m, d = h.shape
    n = w_qkv.shape[2]
    b = m // t
    kw = cw.shape[0]
    tn = _tile(key_a, 512)
    tm, nseg, seg, sub, tiles_per_batch = _conv_proj_tiling(m, t, 1024)
    state_spec = pl.BlockSpec((nseg, kw - 1, tn), lambda j, i: ((i * tm) // (nseg * t), 0, j))
    return pl.pallas_call(
        functools.partial(_qkv_proj_kernel, n_qk_tiles=2 * key_a // tn, q_tiles=key_a // tn, seg=seg, nseg=nseg,
                          tiles_per_batch=tiles_per_batch, sub=sub),
        out_shape=(jax.ShapeDtypeStruct((m, n), F32), jax.ShapeDtypeStruct((b, kw - 1, n), F32)),
        grid=(n // tn, m // tm),
        in_specs=[
            pl.BlockSpec((tm, d), lambda j, i: (i, 0)),
            pl.BlockSpec((None, d, tn), lambda j, i: (layer, 0, j)),
            state_spec,
            pl.BlockSpec((kw, tn), lambda j, i: (0, j)),
        ],
        out_specs=(pl.BlockSpec((tm, tn), lambda j, i: (i, j)), state_spec),
        scratch_shapes=[pltpu.VMEM((nseg * (HALO + seg), tn), F32)],
        compiler_params=_params("parallel", "arbitrary"),
        name="qkv_proj",
    )(h, w_qkv, state, cw)


def _block_diag(m):
    z = jnp.zeros((m.shape[0], HEAD_DIM), m.dtype)
    return jnp.concatenate([jnp.concatenate([m[:, :HEAD_DIM], z], axis=1),
                            jnp.concatenate([z, m[:, HEAD_DIM:]], axis=1)], axis=0)


def _delta_prep_kernel(q_ref, k_ref, v_ref, ab_ref, alog_ref, dtb_ref, lvl_ref, u_ref, wq_ref, qkk_ref, cd_ref,
                       t_ref, a_ref, rhs_ref, *, n_heads, t_valid):
    L, D, W = DELTA_CHUNK, HEAD_DIM, 2 * HEAD_DIM
    n_pairs = n_heads // 2
    n_levels = lvl_ref.shape[0]
    t = pl.program_id(1)
    ab = ab_ref[0]
    valid = (t * L + lax.broadcasted_iota(jnp.int32, ab.shape, 0)) < t_valid
    g = jnp.where(valid, -jnp.exp(alog_ref[...]) * _softplus(ab + dtb_ref[...]), 0.0)
    beta = jnp.where(valid, jax.nn.sigmoid(ab), 0.0)
    lane = lax.broadcasted_iota(jnp.int32, ab.shape, 1)
    tri = (lax.broadcasted_iota(jnp.int32, (L, L), 0) >= lax.broadcasted_iota(jnp.int32, (L, L), 1)).astype(F32)
    cum = jnp.dot(tri, g, precision=lax.Precision.HIGHEST, preferred_element_type=F32)
    cum_t = cum.T
    ii = lax.broadcasted_iota(jnp.int32, (L, W), 0)
    jj = lax.broadcasted_iota(jnp.int32, (L, W), 1) & (D - 1)
    causal = ii >= jj
    strict = ii > jj
    eye = jnp.where(ii == jj, 1.0, 0.0)
    second = lax.broadcasted_iota(jnp.int32, (1, W), 1) >= D

    def pair_cols(f):
        return jnp.where(second, f(1), f(0))

    for p in range(n_pairs):
        cols = slice(p * W, (p + 1) * W)
        pick = lambda x, idx: jnp.sum(jnp.where(lane == idx, x, 0.0), axis=1, keepdims=True)
        cum_col = pair_cols(lambda e: pick(cum, 2 * p + e))
        b_col = pair_cols(lambda e: pick(beta, n_heads + 2 * p + e))
        cum_row = jnp.concatenate([cum_t[2 * p:2 * p + 1, :], cum_t[2 * p + 1:2 * p + 2, :]], axis=1)
        g_last = cum_col[L - 1:L, :]
        gamma = jnp.where(causal, jnp.exp(jnp.where(causal, cum_col - cum_row, 0.0)), 0.0)
        e_col = jnp.exp(cum_col)

        q, k, v = q_ref[0, :, cols], k_ref[0, :, cols], v_ref[0, :, cols]
        kb = k * b_col
        kq = lax.dot_general(jnp.concatenate([kb, q], axis=0).astype(BF16), _block_diag(k.astype(BF16)),
                             (((1,), (1,)), ((), ())), preferred_element_type=F32)
        a_mat = jnp.where(strict, kq[:L] * gamma, 0.0)
        a_ref[p] = a_mat
        t_ref[p] = eye - a_mat * lvl_ref[0]
        vb, kbe = v * b_col, kb * e_col
        rhs_ref[p] = jnp.concatenate([vb[:, :D], kbe[:, :D], vb[:, D:], kbe[:, D:]], axis=1)
        wq_ref[0, L:2 * L, cols] = (q * e_col).astype(BF16)
        qkk_ref[0, 0:L, cols] = (kq[L:] * gamma).astype(BF16)
        kd = k * jnp.exp(g_last - cum_col)
        qkk_ref[0, L:2 * L, cols] = jnp.concatenate([kd[:, :D].T, kd[:, D:].T], axis=1).astype(BF16)
        cd_ref[0, 0, :, cols] = jnp.broadcast_to(jnp.exp(g_last), (SUBLANE, W))

    for s in range(1, n_levels):
        for p in range(n_pairs):
            tm = t_ref[p]
            tb = tm.astype(BF16)
            a_s = (a_ref[p] * lvl_ref[s]).astype(BF16)
            pm = jnp.dot(tb, _block_diag(a_s), preferred_element_type=F32)
            qm = jnp.dot(pm.astype(BF16), _block_diag(tb), preferred_element_type=F32)
            t_ref[p] = tm - qm

    for p in range(n_pairs):
        xb = (t_ref[p] - eye).astype(BF16)
        for e in range(2):
            hs = slice((2 * p + e) * D, (2 * p + e + 1) * D)
            rhs = rhs_ref[p, :, e * W:(e + 1) * W]
            sol = rhs + jnp.dot(xb[:, e * D:(e + 1) * D], rhs.astype(BF16), preferred_element_type=F32)
            u_ref[0, :, hs] = sol[:, :D]
            wq_ref[0, 0:L, hs] = sol[:, D:].astype(BF16)


def _level_masks():
    L = DELTA_CHUNK
    ii = jnp.arange(L)[:, None]
    jj = jnp.arange(2 * HEAD_DIM)[None, :] % HEAD_DIM
    return jnp.stack([(((ii >> s) ^ (jj >> s)) == 1) for s in range(L.bit_length() - 1)]).astype(F32)


def _delta_prep(qkv, ab, a_log, dt_bias, *, n_heads, t_valid):
    b, t, _ = qkv.shape
    L = DELTA_CHUNK
    assert L == HEAD_DIM and n_heads % 2 == 0
    hw = n_heads * HEAD_DIM
    nc = t // L
    lvl = _level_masks()
    pad = lambda a: jnp.pad(a.reshape(1, n_heads), ((0, 0), (0, LANE - n_heads)))
    tok = lambda off: pl.BlockSpec((1, L, hw), lambda i, j: (i, j, off))
    row = pl.BlockSpec((1, LANE), lambda i, j: (0, 0))
    pair_scratch = lambda width: pltpu.VMEM((n_heads // 2, L, width), F32)
    return pl.pallas_call(
        functools.partial(_delta_prep_kernel, n_heads=n_heads, t_valid=t_valid),
        out_shape=(
            jax.ShapeDtypeStruct((b, t, hw), F32),
            jax.ShapeDtypeStruct((b, 2 * t, hw), BF16),
            jax.ShapeDtypeStruct((b, 2 * t, hw), BF16),
            jax.ShapeDtypeStruct((b, nc, SUBLANE, hw), F32),
        ),
        grid=(b, nc),
        in_specs=[tok(0), tok(1), tok(2), pl.BlockSpec((1, L, LANE), lambda i, j: (i, j, 0)), row, row,
                  pl.BlockSpec(lvl.shape, lambda i, j: (0, 0, 0))],
        out_specs=(
            pl.BlockSpec((1, L, hw), lambda i, j: (i, j, 0)),
            pl.BlockSpec((1, 2 * L, hw), lambda i, j: (i, j, 0)),
            pl.BlockSpec((1, 2 * L, hw), lambda i, j: (i, j, 0)),
            pl.BlockSpec((1, 1, SUBLANE, hw), lambda i, j: (i, j, 0, 0)),
        ),
        scratch_shapes=[pair_scratch(2 * HEAD_DIM), pair_scratch(2 * HEAD_DIM), pair_scratch(4 * HEAD_DIM)],
        compiler_params=_params("parallel", "parallel"),
        name="delta_prep",
    )(qkv, qkv, qkv, ab, pad(a_log), pad(dt_bias), lvl)


def _delta_scan_kernel(u_ref, wq_ref, qkk_ref, cd_ref, z_ref, gn_ref, s0_ref, o_ref, so_ref, s_ref,
                       *, hb, nc, rows_out):
    L = DELTA_CHUNK
    t = pl.program_id(2)

    @pl.when(t == 0)
    def _():
        s_ref[...] = s0_ref[0]

    for c in range(nc):
        for hh in range(hb):
            hs = slice(hh * HEAD_DIM, (hh + 1) * HEAD_DIM)
            s = s_ref[hh]
            x1 = jnp.dot(wq_ref[0, 2 * c * L:2 * (c + 1) * L, hs], s.astype(BF16), preferred_element_type=F32)
            v_new = u_ref[0, c * L:(c + 1) * L, hs] - x1[:L]
            x2 = jnp.dot(qkk_ref[0, 2 * c * L:2 * (c + 1) * L, hs], v_new.astype(BF16),
                         preferred_element_type=F32)
            s_ref[hh] = s * cd_ref[0, c, 0:1, hs] + x2[L:]
            o = (x1[L:] + x2[:L])[:rows_out]
            z = z_ref[0, c * rows_out:(c + 1) * rows_out, hs]
            o = o * lax.rsqrt(jnp.mean(o * o, axis=-1, keepdims=True) + EPS) * gn_ref[...] * _silu(z)
            o_ref[0, c * rows_out:(c + 1) * rows_out, hs] = o.astype(o_ref.dtype)

    @pl.when(t == pl.num_programs(2) - 1)
    def _():
        so_ref[0] = s_ref[...]


def _delta_scan(u, wq, qkk, cd, proj, z_off, gnorm, s0, *, t_real):
    b, t, hw = u.shape
    L = DELTA_CHUNK
    n_heads = hw // HEAD_DIM
    hb = min(n_heads, 8)
    hbw = hb * HEAD_DIM
    assert z_off % hbw == 0
    nc = min(t // L, 4)
    rows_out = L if t_real == t else t_real
    assert t_real == t or t == L
    nt = t // (nc * L)
    return pl.pallas_call(
        functools.partial(_delta_scan_kernel, hb=hb, nc=nc, rows_out=rows_out),
        out_shape=(jax.ShapeDtypeStruct((b, t_real, hw), BF16), jax.ShapeDtypeStruct(s0.shape, F32)),
        grid=(b, n_heads // hb, nt),
        in_specs=[
            pl.BlockSpec((1, nc * L, hbw), lambda i, h, j: (i, j, h)),
            pl.BlockSpec((1, 2 * nc * L, hbw), lambda i, h, j: (i, j, h)),
            pl.BlockSpec((1, 2 * nc * L, hbw), lambda i, h, j: (i, j, h)),
            pl.BlockSpec((1, nc, SUBLANE, hbw), lambda i, h, j: (i, j, 0, h)),
            pl.BlockSpec((1, nc * rows_out, hbw), lambda i, h, j: (i, j, z_off // hbw + h)),
            pl.BlockSpec((1, HEAD_DIM), lambda i, h, j: (0, 0)),
            pl.BlockSpec((1, hb, HEAD_DIM, HEAD_DIM), lambda i, h, j: (i, h, 0, 0)),
        ],
        out_specs=(
            pl.BlockSpec((1, nc * rows_out, hbw), lambda i, h, j: (i, j, h)),
            pl.BlockSpec((1, hb, HEAD_DIM, HEAD_DIM), lambda i, h, j: (i, h, 0, 0)),
        ),
        scratch_shapes=[pltpu.VMEM((hb, HEAD_DIM, HEAD_DIM), F32)],
        compiler_params=_params("parallel", "parallel", "arbitrary"),
        name="delta_scan",
    )(u, wq, qkk, cd, proj, gnorm.reshape(1, HEAD_DIM), s0)


def _lru_kernel(x_ref, gate_ref, st_ref, h0_ref, cw_ref, cb_ref, wr_ref, wi_ref, br_ref, bi_ref, lam_ref,
                y_ref, nb_ref, hn_ref, xp_ref, hc_ref, *, tt, first, blk):
    t = pl.program_id(2)
    nt = pl.num_programs(2)
    hl = cw_ref.shape[0] - 1

    @pl.when(t == 0)
    def _():
        xp_ref[HALO - hl:HALO, :] = st_ref[0]
        hc_ref[...] = h0_ref[0]

    @pl.when(t > 0)
    def _():
        xp_ref[HALO - hl:HALO, :] = xp_ref[HALO + tt - hl:HALO + tt, :]

    xp_ref[HALO:HALO + tt, :] = x_ref[0]
    xc = _conv_from_scratch(xp_ref, cw_ref, tt) + cb_ref[...]
    r_parts, i_parts = [], []
    for n in range(xc.shape[1] // blk):
        xn = xc[:, n * blk:(n + 1) * blk].astype(BF16)
        r_parts.append(jnp.dot(xn, wr_ref[n], preferred_element_type=F32))
        i_parts.append(jnp.dot(xn, wi_ref[n], preferred_element_type=F32))
    r = jax.nn.sigmoid(jnp.concatenate(r_parts, axis=1) + br_ref[...])
    gi = jax.nn.sigmoid(jnp.concatenate(i_parts, axis=1) + bi_ref[...])
    log_a = -C_LRU * r * _softplus(-lam_ref[...])
    a = jnp.exp(log_a)
    th = jnp.tanh(log_a)
    mult = jnp.sqrt(-2.0 * th / (1.0 - th))
    if first:
        reset = (lax.broadcasted_iota(jnp.int32, a.shape, 0) == 0) & (t == 0)
        a = jnp.where(reset, 0.0, a)
        mult = jnp.where(reset, 1.0, mult)
    u = mult * gi * xc

    grouped = (tt // SUBLANE, SUBLANE, a.shape[1])
    a, u = a.reshape(grouped), u.reshape(grouped)
    in_group = lax.broadcasted_iota(jnp.int32, grouped, 1)
    s = 1
    while s < SUBLANE:
        keep = in_group >= s
        u = a * jnp.where(keep, pltpu.roll(u, s, axis=1), 0.0) + u
        a = a * jnp.where(keep, pltpu.roll(a, s, axis=1), 1.0)
        s *= 2
    a, u = a.reshape(tt, -1), u.reshape(tt, -1)
    carry = hc_ref[...]
    groups = []
    for r in range(tt // SUBLANE):
        rows = slice(r * SUBLANE, (r + 1) * SUBLANE)
        groups.append(u[rows] + a[rows] * carry)
        carry = groups[-1][SUBLANE - 1:SUBLANE, :]
    h = jnp.concatenate(groups, axis=0)
    hc_ref[...] = carry
    y_ref[0] = (h * jax.nn.gelu(gate_ref[0])).astype(y_ref.dtype)

    @pl.when(t == nt - 1)
    def _():
        nb_ref[0] = xp_ref[HALO + tt - hl:HALO + tt, :]
        hn_ref[0] = h[tt - 1:tt, :]


def _lru(proj, x_off, gate_off, st_conv, h0, cw, cb, wr, wi, layer, br, bi, lam, *, first):
    b, t, _ = proj.shape
    _, nblk, blk, _ = wr.shape
    w = nblk * blk
    tt = _tile(t, 256)
    tc = _tile(w, 512)
    assert x_off % tc == 0 and gate_off % tc == 0 and tc % blk == 0
    kw = cw.shape[0]
    col = lambda a: a.reshape(1, w)
    rowspec = pl.BlockSpec((1, tc), lambda i, c, j: (0, c))
    return pl.pallas_call(
        functools.partial(_lru_kernel, tt=tt, first=first, blk=blk),
        out_shape=(
            jax.ShapeDtypeStruct((b, t, w), BF16),
            jax.ShapeDtypeStruct((b, kw - 1, w), F32),
            jax.ShapeDtypeStruct((b, 1, w), F32),
        ),
        grid=(b, w // tc, t // tt),
        in_specs=[
            pl.BlockSpec((1, tt, tc), lambda i, c, j: (i, j, x_off // tc + c)),
            pl.BlockSpec((1, tt, tc), lambda i, c, j: (i, j, gate_off // tc + c)),
            pl.BlockSpec((1, kw - 1, tc), lambda i, c, j: (i, 0, c)),
            pl.BlockSpec((1, 1, tc), lambda i, c, j: (i, 0, c)),
            pl.BlockSpec((kw, tc), lambda i, c, j: (0, c)),
            rowspec,
            pl.BlockSpec((None, tc // blk, blk, blk), lambda i, c, j: (layer, c, 0, 0)),
            pl.BlockSpec((None, tc // blk, blk, blk), lambda i, c, j: (layer, c, 0, 0)),
            rowspec, rowspec, rowspec,
        ],
        out_specs=(
            pl.BlockSpec((1, tt, tc), lambda i, c, j: (i, j, c)),
            pl.BlockSpec((1, kw - 1, tc), lambda i, c, j: (i, 0, c)),
            pl.BlockSpec((1, 1, tc), lambda i, c, j: (i, 0, c)),
        ),
        scratch_shapes=[pltpu.VMEM((HALO + tt, tc), F32), pltpu.VMEM((1, tc), F32)],
        compiler_params=_params("parallel", "parallel", "arbitrary"),
        name="rg_lru",
    )(proj, proj, st_conv, h0.reshape(b, 1, w), cw, col(cb), wr, wi, col(br), col(bi), col(lam))


def _ffn_up_kernel(h_ref, wg_ref, wv_ref, st_ref, cw_ref, cb_ref, act_ref, nb_ref, scr_ref, **tiling):
    def project(h):
        return (jnp.dot(h, wg_ref[...], preferred_element_type=F32),
                jnp.dot(h, wv_ref[...], preferred_element_type=F32))

    def emit(rows, conv, value):
        act_ref[rows, :] = (jax.nn.gelu(conv + cb_ref[...]) * value).astype(act_ref.dtype)

    _conv_proj_rows(h_ref, st_ref, cw_ref, nb_ref, scr_ref, project, emit, **tiling)


def _ffn_up(h, w_up, layer, state, cw, cb, *, t):
    m, d = h.shape
    f = w_up.shape[2] // 2
    b = m // t
    kw = cw.shape[0]
    tn = _tile(f, 512)
    tm, nseg, seg, sub, tiles_per_batch = _conv_proj_tiling(m, t, 1024)
    return pl.pallas_call(
        functools.partial(_ffn_up_kernel, seg=seg, nseg=nseg, tiles_per_batch=tiles_per_batch, sub=sub),
        out_shape=(jax.ShapeDtypeStruct((m, f), BF16), jax.ShapeDtypeStruct((b, kw - 1, f), F32)),
        grid=(f // tn, m // tm),
        in_specs=[
            pl.BlockSpec((tm, d), lambda j, i: (i, 0)),
            pl.BlockSpec((None, d, tn), lambda j, i: (layer, 0, j)),
            pl.BlockSpec((None, d, tn), lambda j, i: (layer, 0, f // tn + j)),
            pl.BlockSpec((nseg, kw - 1, tn), lambda j, i: ((i * tm) // (nseg * t), 0, j)),
            pl.BlockSpec((kw, tn), lambda j, i: (0, j)),
            pl.BlockSpec((1, tn), lambda j, i: (0, j)),
        ],
        out_specs=(
            pl.BlockSpec((tm, tn), lambda j, i: (i, j)),
            pl.BlockSpec((nseg, kw - 1, tn), lambda j, i: ((i * tm) // (nseg * t), 0, j)),
        ),
        scratch_shapes=[pltpu.VMEM((nseg * (HALO + seg), tn), F32)],
        compiler_params=_params("parallel", "arbitrary"),
        name="ffn_up",
    )(h, w_up, w_up, state, cw, cb.reshape(1, f))


def _pad_rows(a, rows):
    return a if a.shape[1] == rows else jnp.pad(a, ((0, 0), (0, rows - a.shape[1]), (0, 0)))


def _trunk(x, mod_all, st_delta, st_qkv, st_lru, st_lruconv, st_ffn, first, wts):
    b, t, d = x.shape
    m = b * t
    n_heads = wts["a_log"].shape[1]
    key_a = n_heads * HEAD_DIM
    qkv_a = 3 * key_a
    w_lru = wts["lru_lambda"].shape[1]
    depth = wts["a_log"].shape[0]
    z_off = 0
    xb_off = z_off + key_a
    gb_off = xb_off + w_lru
    ga_off = gb_off + w_lru
    gbl_off = ga_off + d
    t_pad = -(-t // DELTA_CHUNK) * DELTA_CHUNK
    tm_big = 1024
    outs = {k: [] for k in ("delta", "qkv", "lru", "lruconv", "ffn")}

    for l in range(depth):
        sh1, sc1, gt1, sh2, sc2, gt2 = [c.reshape(b, 1, d) for c in jnp.split(mod_all[l], 6, axis=-1)]
        h = _rms_norm(x, wts["norm1_g"][l], sc1, sh1, out_dtype=BF16).reshape(m, d)
        qkv_c, new_qkv = _qkv_proj(h, wts["w_qkv"], l, st_qkv[l], wts["conv_qkv_w"][l], t=t, key_a=key_a)
        proj = _matmul(h, wts["w_rest"], l, out_dtype=F32, tm=tm_big, tn=1024, tk=d, name="in_proj")
        ab = _matmul(h, wts["w_ab"], l, out_dtype=F32, tm=tm_big, tn=LANE, tk=d, name="in_proj_gates")
        proj3 = proj.reshape(b, t, -1)

        qkv_c = qkv_c.reshape(b, t, qkv_a)
        u, wq, qkk, cd = _delta_prep(_pad_rows(qkv_c, t_pad), _pad_rows(ab.reshape(b, t, LANE), t_pad),
                                     wts["a_log"][l], wts["dt_bias"][l], n_heads=n_heads, t_valid=t)
        o_a, new_delta = _delta_scan(u, wq, qkk, cd, proj3, z_off, wts["gnorm_a"][l], st_delta[l], t_real=t)

        y_lru, new_lruconv, new_lru = _lru(
            proj3, xb_off, gb_off, st_lruconv[l], st_lru[l], wts["conv_lru_w"][l], wts["conv_lru_b"][l],
            wts["lru_wr"], wts["lru_wi"], l, wts["lru_br"][l], wts["lru_bi"][l], wts["lru_lambda"][l],
            first=first)

        mixed = _mixer_merge(o_a.reshape(m, key_a), y_lru.reshape(m, w_lru), wts["w_ao"], wts["w_bo"], l,
                             proj, ga_off, gbl_off)
        tm, tn = _tile(m, 512), _tile(d, 1024)
        x2 = x.reshape(m, d)
        x2 = _matmul(mixed, wts["w_o"], l, out_dtype=F32, tm=tm, tn=tn, tk=d, n_outer=True,
                     epilogue=functools.partial(_residual_epilogue_multi, t=t),
                     extras=[(x2, (tm, tn), lambda i, j: (i, j)), _gate_extra(gt1, t, tm, tn)], name="proj_o")

        h = _rms_norm(x2.reshape(b, t, d), wts["norm2_g"][l], sc2, sh2, out_dtype=BF16).reshape(m, d)
        act, new_ffn = _ffn_up(h, wts["w_up"], l, st_ffn[l], wts["conv_ffn_w"][l], wts["conv_ffn_b"][l], t=t)
        tm, tn = _tile(m, 256), _tile(d, 512)
        x2 = _matmul(act, wts["w_down"], l, out_dtype=F32, tm=tm, tn=tn, tk=act.shape[1], n_outer=True,
                     epilogue=functools.partial(_residual_epilogue_multi, t=t),
                     extras=[(x2, (tm, tn), lambda i, j: (i, j)), _gate_extra(gt2, t, tm, tn)], name="ffn_down")
        x = x2.reshape(b, t, d)

        outs["delta"].append(new_delta)
        outs["qkv"].append(new_qkv)
        outs["lru"].append(new_lru.reshape(b, w_lru))
        outs["lruconv"].append(new_lruconv)
        outs["ffn"].append(new_ffn)

    y = _rms_norm(x, wts["final_g"], out_dtype=F32)
    return (y,) + tuple(jnp.stack(outs[k]) for k in ("delta", "qkv", "lru", "lruconv", "ffn"))


def kernel(x_prompt, x_sample, state_delta, state_qkv_conv, state_lru, state_lru_conv, state_ffn_conv,
           c_prompt, c_sample, norm1_g, w_ada, b_ada, w_in, conv_qkv_w, a_log, dt_bias, gnorm_a, w_ao,
           conv_lru_w, conv_lru_b, lru_wr, lru_br, lru_wi, lru_bi, lru_lambda, w_bo, w_o, norm2_g, w_up,
           conv_ffn_w, conv_ffn_b, w_down, final_g):
    depth, n_heads = a_log.shape
    bp, bs = x_prompt.shape[0], x_sample.shape[0]
    qkv_a = 3 * n_heads * HEAD_DIM
    gate_cols = w_in[:, :, qkv_a:qkv_a + 2 * n_heads]
    wts = dict(
        w_qkv=w_in[:, :, :qkv_a].astype(BF16),
        w_rest=w_in[:, :, qkv_a + 2 * n_heads:].astype(BF16),
        w_ab=jnp.pad(gate_cols, ((0, 0), (0, 0), (0, LANE - 2 * n_heads))).astype(BF16),
        w_ao=w_ao.astype(BF16), w_bo=w_bo.astype(BF16), w_o=w_o.astype(BF16), w_up=w_up.astype(BF16),
        w_down=w_down.astype(BF16), lru_wr=lru_wr.astype(BF16), lru_wi=lru_wi.astype(BF16),
        norm1_g=norm1_g, conv_qkv_w=conv_qkv_w, a_log=a_log, dt_bias=dt_bias, gnorm_a=gnorm_a,
        conv_lru_w=conv_lru_w, conv_lru_b=conv_lru_b, lru_br=lru_br, lru_bi=lru_bi, lru_lambda=lru_lambda,
        norm2_g=norm2_g, conv_ffn_w=conv_ffn_w, conv_ffn_b=conv_ffn_b, final_g=final_g)

    rows = -(-(bp + bs) // 16) * 16
    c_all = jnp.pad(jnp.concatenate([c_prompt, c_sample], axis=0), ((0, rows - bp - bs), (0, 0)))
    mod_all = _ada_mod(c_all, w_ada, b_ada)

    dt = x_prompt.dtype
    zeros = lambda a: jnp.zeros((depth, bp) + a.shape[2:], a.dtype)
    prompt = _trunk(x_prompt, mod_all[:, :bp], zeros(state_delta).astype(F32), zeros(state_qkv_conv),
                    zeros(state_lru).astype(F32), zeros(state_lru_conv), zeros(state_ffn_conv), True, wts)
    sample = _trunk(x_sample, mod_all[:, bp:bp + bs], state_delta, state_qkv_conv, state_lru, state_lru_conv,
                    state_ffn_conv, False, wts)
    cast = lambda outs: tuple(o.astype(dt) for o in outs)
    return (prompt[0], sample[0]) + cast(prompt[1:]) + cast(sample[1:])
```

```python
import functools
import math

import jax
import jax.numpy as jnp
from jax import lax
from jax.experimental import pallas as pl
from jax.experimental.pallas import tpu as pltpu

F32 = jnp.float32
BF16 = jnp.bfloat16

LANE = 128
SUBLANE = 8
VMEM_LIMIT_BYTES = 56 * 1024 * 1024

EPS = 1e-6
C_LRU = 8.0
HEAD_DIM = 128
DELTA_CHUNK = 128
HALO = SUBLANE


def _params(*sem):
    return pltpu.CompilerParams(dimension_semantics=sem, vmem_limit_bytes=VMEM_LIMIT_BYTES)


def _tile(dim, pref):
    if dim <= pref:
        return dim
    for t in range(pref - pref % LANE, 0, -LANE):
        if dim % t == 0:
            return t
    raise ValueError((dim, pref))


def _silu(x):
    return x * jax.nn.sigmoid(x)


def _softplus(x):
    return jnp.maximum(x, 0.0) + jnp.log1p(jnp.exp(-jnp.abs(x)))


def _bdot(a, b):
    return jnp.dot(a.astype(BF16), b.astype(BF16), preferred_element_type=F32)


def _ada_kernel(c_ref, w_ref, b_ref, o_ref):
    o_ref[0] = _bdot(_silu(c_ref[...]), w_ref[0]) + b_ref[0]


def _ada_mod(c_all, w_ada, b_ada):
    rows, d = c_all.shape
    depth, _, n = w_ada.shape
    tn = _tile(n, 512)
    return pl.pallas_call(
        _ada_kernel,
        out_shape=jax.ShapeDtypeStruct((depth, rows, n), F32),
        grid=(depth, n // tn),
        in_specs=[
            pl.BlockSpec((rows, d), lambda l, j: (0, 0)),
            pl.BlockSpec((1, d, tn), lambda l, j: (l, 0, j)),
            pl.BlockSpec((1, 1, tn), lambda l, j: (l, 0, j)),
        ],
        out_specs=pl.BlockSpec((1, rows, tn), lambda l, j: (l, 0, j)),
        compiler_params=_params("parallel", "parallel"),
        name="ada_mod",
    )(c_all, w_ada, b_ada.reshape(depth, 1, n))


def _cast_cols_kernel(*refs, shift):
    a = refs[0][...]
    if shift:
        a = jnp.concatenate([a[:, shift:], refs[1][:, :shift]], axis=1)
    refs[-1][...] = a.astype(refs[-1].dtype)


def _cast_cols(w, start, width):
    depth, k, n = w.shape
    shift = start % LANE
    base = start - shift
    tn = _tile(math.gcd(width, base) if base else width, 1024)
    tk = _tile(k, 512)
    assert width % tn == 0 and base % tn == 0 and tn % LANE == 0 and start + width <= n
    in_specs = [pl.BlockSpec((None, tk, tn), lambda l, i, j: (l, i, base // tn + j))]
    if shift:
        in_specs.append(pl.BlockSpec((None, tk, LANE), lambda l, i, j: (l, i, (base + (j + 1) * tn) // LANE)))
    return pl.pallas_call(
        functools.partial(_cast_cols_kernel, shift=shift),
        out_shape=jax.ShapeDtypeStruct((depth, k, width), BF16),
        grid=(depth, k // tk, width // tn),
        in_specs=in_specs,
        out_specs=pl.BlockSpec((None, tk, tn), lambda l, i, j: (l, i, j)),
        compiler_params=_params("parallel", "parallel", "parallel"),
        name="cast_cols",
    )(*([w] * len(in_specs)))


def _norm_kernel(*refs, modulate):
    if modulate:
        x_ref, g_ref, sc_ref, sh_ref, o_ref = refs
    else:
        x_ref, g_ref, o_ref = refs
    x = x_ref[0]
    y = x * lax.rsqrt(jnp.mean(x * x, axis=-1, keepdims=True) + EPS)
    if modulate:
        y = y * (g_ref[...] * (1.0 + sc_ref[0])) + sh_ref[0]
    else:
        y = y * g_ref[...]
    o_ref[0] = y.astype(o_ref.dtype)


def _rms_norm(x, g, sc=None, sh=None, *, out_dtype):
    b, t, d = x.shape
    tt = _tile(t, 512)
    modulate = sc is not None
    row = pl.BlockSpec((1, 1, d), lambda i, j: (i, 0, 0))
    in_specs = [pl.BlockSpec((1, tt, d), lambda i, j: (i, j, 0)), pl.BlockSpec((1, d), lambda i, j: (0, 0))]
    args = [x, g.reshape(1, d)]
    if modulate:
        in_specs += [row, row]
        args += [sc, sh]
    return pl.pallas_call(
        functools.partial(_norm_kernel, modulate=modulate),
        out_shape=jax.ShapeDtypeStruct((b, t, d), out_dtype),
        grid=(b, t // tt),
        in_specs=in_specs,
        out_specs=pl.BlockSpec((1, tt, d), lambda i, j: (i, j, 0)),
        compiler_params=_params("parallel", "parallel"),
        name="rms_norm_mod" if modulate else "rms_norm",
    )(*args)


def _matmul_kernel(*refs, n_extra, nk, epilogue):
    a_ref, b_ref = refs[:2]
    extra = refs[2:2 + n_extra]
    o_ref = refs[2 + n_extra]

    def finish(acc):
        o_ref[...] = epilogue(acc, *[e[...] for e in extra]).astype(o_ref.dtype)

    part = jnp.dot(a_ref[...], b_ref[...], preferred_element_type=F32)
    if nk == 1:
        finish(part)
        return
    acc_ref = refs[3 + n_extra]
    k = pl.program_id(2)

    @pl.when(k == 0)
    def _():
        acc_ref[...] = part

    @pl.when(k > 0)
    def _():
        acc_ref[...] += part

    @pl.when(k == nk - 1)
    def _():
        finish(acc_ref[...])


def _matmul(a, b, layer, *, out_dtype, tm, tn, tk, epilogue=None, extras=(), n_outer=False, name):
    m, k = a.shape
    _, _, n = b.shape
    tm, tn, tk = _tile(m, tm), _tile(n, tn), _tile(k, tk)
    nk = k // tk
    if epilogue is None:
        epilogue = lambda acc: acc
    if n_outer:
        grid = (n // tn, m // tm, nk)
        order = lambda f: (lambda j, i, kk: f(i, j, kk))
    else:
        grid = (m // tm, n // tn, nk)
        order = lambda f: f
    in_specs = [pl.BlockSpec((tm, tk), order(lambda i, j, kk: (i, kk))),
                pl.BlockSpec((None, tk, tn), order(lambda i, j, kk: (layer, kk, j)))]
    for _, shape, imap in extras:
        in_specs.append(pl.BlockSpec(shape, order(functools.partial(lambda i, j, kk, f: f(i, j), f=imap))))
    return pl.pallas_call(
        functools.partial(_matmul_kernel, n_extra=len(extras), nk=nk, epilogue=epilogue),
        out_shape=jax.ShapeDtypeStruct((m, n), out_dtype),
        grid=grid,
        in_specs=in_specs,
        out_specs=pl.BlockSpec((tm, tn), order(lambda i, j, kk: (i, j))),
        scratch_shapes=[pltpu.VMEM((tm, tn), F32)] if nk > 1 else [],
        compiler_params=_params("parallel", "parallel", "arbitrary"),
        name=name,
    )(a, b, *[e[0] for e in extras])


def _mixer_merge_kernel(oa_ref, ob_ref, wa_ref, wb_ref, ga_ref, gb_ref, o_ref):
    y_a = jnp.dot(oa_ref[...], wa_ref[...], preferred_element_type=F32)
    y_b = jnp.dot(ob_ref[...], wb_ref[...], preferred_element_type=F32)
    o_ref[...] = (jax.nn.sigmoid(ga_ref[...]) * y_a + jax.nn.sigmoid(gb_ref[...]) * y_b).astype(o_ref.dtype)


def _mixer_merge(o_a, o_b, w_ao, w_bo, layer, proj, ga_off, gb_off):
    m, ka = o_a.shape
    _, kb = o_b.shape
    n = w_ao.shape[2]
    tm, tn = _tile(m, 512), _tile(math.gcd(n, ga_off, gb_off), 1024)
    return pl.pallas_call(
        _mixer_merge_kernel,
        out_shape=jax.ShapeDtypeStruct((m, n), BF16),
        grid=(n // tn, m // tm),
        in_specs=[
            pl.BlockSpec((tm, ka), lambda j, i: (i, 0)),
            pl.BlockSpec((tm, kb), lambda j, i: (i, 0)),
            pl.BlockSpec((None, ka, tn), lambda j, i: (layer, 0, j)),
            pl.BlockSpec((None, kb, tn), lambda j, i: (layer, 0, j)),
            pl.BlockSpec((tm, tn), lambda j, i: (i, ga_off // tn + j)),
            pl.BlockSpec((tm, tn), lambda j, i: (i, gb_off // tn + j)),
        ],
        out_specs=pl.BlockSpec((tm, tn), lambda j, i: (i, j)),
        compiler_params=_params("parallel", "parallel"),
        name="mixer_merge",
    )(o_a, o_b, w_ao, w_bo, proj, proj)


def _gate_extra(gate, t, tm, tn):
    if tm <= t:
        return (gate, (1, 1, tn), lambda i, j: ((i * tm) // t, 0, j))
    assert tm % t == 0
    return (gate, (tm // t, 1, tn), lambda i, j: (i, 0, j))


def _residual_epilogue_multi(acc, x, gate, *, t):
    nb = gate.shape[0]
    if nb == 1:
        return x + gate[0] * acc
    parts = [x[s * t:(s + 1) * t] + gate[s] * acc[s * t:(s + 1) * t] for s in range(nb)]
    return jnp.concatenate(parts, axis=0)


def _conv_from_scratch(xp_ref, w_ref, rows, base=0):
    kw = w_ref.shape[0]
    y = None
    for j in range(kw):
        term = w_ref[j:j + 1, :] * xp_ref[pl.ds(base + HALO - (kw - 1) + j, rows), :]
        y = term if y is None else y + term
    return y


def _conv_proj_rows(h_ref, st_ref, cw_ref, nb_ref, scr_ref, project, emit, *, seg, nseg, tiles_per_batch, sub):
    i = pl.program_id(1)
    hl = cw_ref.shape[0] - 1
    pos = i % tiles_per_batch
    if nseg == 1:
        if tiles_per_batch == 1:
            scr_ref[HALO - hl:HALO, :] = st_ref[0]
        else:
            @pl.when(pos == 0)
            def _():
                scr_ref[HALO - hl:HALO, :] = st_ref[0]

            @pl.when(pos > 0)
            def _():
                scr_ref[HALO - hl:HALO, :] = scr_ref[HALO + seg - hl:HALO + seg, :]

        def start(r):
            x, aux = project(h_ref[r * sub:(r + 1) * sub, :])
            scr_ref[HALO + r * sub:HALO + (r + 1) * sub, :] = x
            return aux

        n_sub = seg // sub
        aux = start(0)
        for r in range(n_sub):
            aux_next = start(r + 1) if r + 1 < n_sub else None
            emit(slice(r * sub, (r + 1) * sub), _conv_from_scratch(scr_ref, cw_ref, sub, r * sub), aux)
            aux = aux_next

        if tiles_per_batch == 1:
            nb_ref[0] = scr_ref[HALO + seg - hl:HALO + seg, :]
        else:
            @pl.when(pos == tiles_per_batch - 1)
            def _():
                nb_ref[0] = scr_ref[HALO + seg - hl:HALO + seg, :]
        return

    x, aux = project(h_ref[...])
    for s in range(nseg):
        base = s * (HALO + seg)
        rows = slice(s * seg, (s + 1) * seg)
        scr_ref[base + HALO - hl:base + HALO, :] = st_ref[s]
        scr_ref[base + HALO:base + HALO + seg, :] = x[rows]
        emit(rows, _conv_from_scratch(scr_ref, cw_ref, seg, base), None if aux is None else aux[rows])
        nb_ref[s] = scr_ref[base + HALO + seg - hl:base + HALO + seg, :]


def _conv_proj_tiling(m, t, tm_pref, sub_pref):
    if t >= 512:
        tm, nseg = _tile(t, tm_pref), 1
    else:
        tm = _tile(m, max(t, 128))
        nseg = tm // t
    seg = tm // nseg
    tiles_per_batch = max(t // tm, 1)
    assert nseg == 1 or tiles_per_batch == 1
    return tm, nseg, seg, _tile(seg, sub_pref), tiles_per_batch


def _qkv_proj_kernel(h_ref, w_ref, st_ref, cw_ref, o_ref, nb_ref, scr_ref, *, n_qk_tiles, q_tiles, **tiling):
    j = pl.program_id(0)
    is_qk = j < n_qk_tiles
    scale = jnp.where(j < q_tiles, HEAD_DIM ** -0.5, 1.0).astype(F32)

    def project(h):
        return jnp.dot(h, w_ref[...], preferred_element_type=F32), None

    def emit(rows, conv, _):
        y = _silu(conv)
        for hh in range(y.shape[1] // HEAD_DIM):
            hs = slice(hh * HEAD_DIM, (hh + 1) * HEAD_DIM)
            blk = y[:, hs]
            inv = lax.rsqrt(jnp.sum(blk * blk, axis=-1, keepdims=True) + EPS) * scale
            o_ref[rows, hs] = blk * jnp.where(is_qk, inv, 1.0)

    _conv_proj_rows(h_ref, st_ref, cw_ref, nb_ref, scr_ref, project, emit, **tiling)


def _qkv_proj(h, w_qkv, layer, state, cw, *, t, key_a):
    m, d = h.shape
    n = w_qkv.shape[2]
    b = m // t
    kw = cw.shape[0]
    tn = _tile(key_a, 512)
    tm, nseg, seg, sub, tiles_per_batch = _conv_proj_tiling(m, t, 1024, 128)
    state_spec = pl.BlockSpec((nseg, kw - 1, tn), lambda j, i: ((i * tm) // (nseg * t), 0, j))
    return pl.pallas_call(
        functools.partial(_qkv_proj_kernel, n_qk_tiles=2 * key_a // tn, q_tiles=key_a // tn, seg=seg, nseg=nseg,
                          tiles_per_batch=tiles_per_batch, sub=sub),
        out_shape=(jax.ShapeDtypeStruct((m, n), F32), jax.ShapeDtypeStruct((b, kw - 1, n), F32)),
        grid=(n // tn, m // tm),
        in_specs=[
            pl.BlockSpec((tm, d), lambda j, i: (i, 0)),
            pl.BlockSpec((None, d, tn), lambda j, i: (layer, 0, j)),
            state_spec,
            pl.BlockSpec((kw, tn), lambda j, i: (0, j)),
        ],
        out_specs=(pl.BlockSpec((tm, tn), lambda j, i: (i, j)), state_spec),
        scratch_shapes=[pltpu.VMEM((nseg * (HALO + seg), tn), F32)],
        compiler_params=_params("parallel", "arbitrary"),
        name="qkv_proj",
    )(h, w_qkv, state, cw)


def _block_diag(m):
    z = jnp.zeros((m.shape[0], HEAD_DIM), m.dtype)
    return jnp.concatenate([jnp.concatenate([m[:, :HEAD_DIM], z], axis=1),
                            jnp.concatenate([z, m[:, HEAD_DIM:]], axis=1)], axis=0)


def _delta_prep_kernel(q_ref, k_ref, v_ref, ab_ref, alog_ref, dtb_ref, lvl_ref, u_ref, wq_ref, qkk_ref, cd_ref,
                       t_ref, a_ref, rhs_ref, *, n_heads, t_valid):
    L, D, W = DELTA_CHUNK, HEAD_DIM, 2 * HEAD_DIM
    n_pairs = n_heads // 2
    n_levels = lvl_ref.shape[0]
    t = pl.program_id(1)
    ab = ab_ref[0]
    valid = (t * L + lax.broadcasted_iota(jnp.int32, ab.shape, 0)) < t_valid
    g = jnp.where(valid, -jnp.exp(alog_ref[...]) * _softplus(ab + dtb_ref[...]), 0.0)
    beta = jnp.where(valid, jax.nn.sigmoid(ab), 0.0)
    lane = lax.broadcasted_iota(jnp.int32, ab.shape, 1)
    tri = (lax.broadcasted_iota(jnp.int32, (L, L), 0) >= lax.broadcasted_iota(jnp.int32, (L, L), 1)).astype(F32)
    cum = jnp.dot(tri, g, precision=lax.Precision.HIGHEST, preferred_element_type=F32)
    cum_t = cum.T
    ii = lax.broadcasted_iota(jnp.int32, (L, W), 0)
    jj = lax.broadcasted_iota(jnp.int32, (L, W), 1) & (D - 1)
    causal = ii >= jj
    strict = ii > jj
    eye = jnp.where(ii == jj, 1.0, 0.0)
    second = lax.broadcasted_iota(jnp.int32, (1, W), 1) >= D

    def pair_cols(f):
        return jnp.where(second, f(1), f(0))

    for p in range(n_pairs):
        cols = slice(p * W, (p + 1) * W)
        pick = lambda x, idx: jnp.sum(jnp.where(lane == idx, x, 0.0), axis=1, keepdims=True)
        cum_col = pair_cols(lambda e: pick(cum, 2 * p + e))
        b_col = pair_cols(lambda e: pick(beta, n_heads + 2 * p + e))
        cum_row = jnp.concatenate([cum_t[2 * p:2 * p + 1, :], cum_t[2 * p + 1:2 * p + 2, :]], axis=1)
        g_last = cum_col[L - 1:L, :]
        gamma = jnp.where(causal, jnp.exp(jnp.where(causal, cum_col - cum_row, 0.0)), 0.0)
        e_col = jnp.exp(cum_col)

        q, k, v = q_ref[0, :, cols], k_ref[0, :, cols], v_ref[0, :, cols]
        kb = k * b_col
        kq = lax.dot_general(jnp.concatenate([kb, q], axis=0).astype(BF16), _block_diag(k.astype(BF16)),
                             (((1,), (1,)), ((), ())), preferred_element_type=F32)
        a_mat = jnp.where(strict, kq[:L] * gamma, 0.0)
        a_ref[p] = a_mat.astype(BF16)
        t_ref[p] = eye - jnp.where((ii ^ jj) == 1, a_mat, 0.0)
        vb, kbe = v * b_col, kb * e_col
        rhs_ref[p] = jnp.concatenate([vb[:, :D], kbe[:, :D], vb[:, D:], kbe[:, D:]], axis=1)
        wq_ref[0, L:2 * L, cols] = (q * e_col).astype(BF16)
        qkk_ref[0, 0:L, cols] = (kq[L:] * gamma).astype(BF16)
        kd = k * jnp.exp(g_last - cum_col)
        qkk_ref[0, L:2 * L, cols] = jnp.concatenate([kd[:, :D].T, kd[:, D:].T], axis=1).astype(BF16)
        cd_ref[0, 0, :, cols] = jnp.broadcast_to(jnp.exp(g_last), (SUBLANE, W))

    for s in range(1, n_levels):
        for p in range(n_pairs):
            tm = t_ref[p]
            tb = tm.astype(BF16)
            pm = jnp.dot(tb, _block_diag(a_ref[p] * lvl_ref[s]), preferred_element_type=F32)
            qm = jnp.dot(pm.astype(BF16), _block_diag(tb), preferred_element_type=F32)
            t_ref[p] = tm - qm

    for p in range(n_pairs):
        xb = (t_ref[p] - eye).astype(BF16)
        for e in range(2):
            hs = slice((2 * p + e) * D, (2 * p + e + 1) * D)
            rhs = rhs_ref[p, :, e * W:(e + 1) * W]
            sol = rhs + jnp.dot(xb[:, e * D:(e + 1) * D], rhs.astype(BF16), preferred_element_type=F32)
            u_ref[0, :, hs] = sol[:, :D]
            wq_ref[0, 0:L, hs] = sol[:, D:].astype(BF16)


def _level_masks():
    L = DELTA_CHUNK
    ii = jnp.arange(L)[:, None]
    jj = jnp.arange(2 * HEAD_DIM)[None, :] % HEAD_DIM
    return jnp.stack([(((ii >> s) ^ (jj >> s)) == 1) for s in range(L.bit_length() - 1)]).astype(BF16)


def _delta_prep(qkv, ab, a_log, dt_bias, *, n_heads, t_valid):
    b, t, _ = qkv.shape
    L = DELTA_CHUNK
    assert L == HEAD_DIM and n_heads % 2 == 0
    hw = n_heads * HEAD_DIM
    nc = t // L
    lvl = _level_masks()
    pad = lambda a: jnp.pad(a.reshape(1, n_heads), ((0, 0), (0, LANE - n_heads)))
    tok = lambda off: pl.BlockSpec((1, L, hw), lambda i, j: (i, j, off))
    row = pl.BlockSpec((1, LANE), lambda i, j: (0, 0))
    pair_scratch = lambda width, dtype=F32: pltpu.VMEM((n_heads // 2, L, width), dtype)
    return pl.pallas_call(
        functools.partial(_delta_prep_kernel, n_heads=n_heads, t_valid=t_valid),
        out_shape=(
            jax.ShapeDtypeStruct((b, t, hw), F32),
            jax.ShapeDtypeStruct((b, 2 * t, hw), BF16),
            jax.ShapeDtypeStruct((b, 2 * t, hw), BF16),
            jax.ShapeDtypeStruct((b, nc, SUBLANE, hw), F32),
        ),
        grid=(b, nc),
        in_specs=[tok(0), tok(1), tok(2), pl.BlockSpec((1, L, LANE), lambda i, j: (i, j, 0)), row, row,
                  pl.BlockSpec(lvl.shape, lambda i, j: (0, 0, 0))],
        out_specs=(
            pl.BlockSpec((1, L, hw), lambda i, j: (i, j, 0)),
            pl.BlockSpec((1, 2 * L, hw), lambda i, j: (i, j, 0)),
            pl.BlockSpec((1, 2 * L, hw), lambda i, j: (i, j, 0)),
            pl.BlockSpec((1, 1, SUBLANE, hw), lambda i, j: (i, j, 0, 0)),
        ),
        scratch_shapes=[pair_scratch(2 * HEAD_DIM), pair_scratch(2 * HEAD_DIM, BF16), pair_scratch(4 * HEAD_DIM)],
        compiler_params=_params("parallel", "parallel"),
        name="delta_prep",
    )(qkv, qkv, qkv, ab, pad(a_log), pad(dt_bias), lvl)


def _delta_scan_kernel(u_ref, wq_ref, qkk_ref, cd_ref, z_ref, gn_ref, s0_ref, o_ref, so_ref, s_ref,
                       *, hb, nc, rows_out):
    L = DELTA_CHUNK
    t = pl.program_id(2)

    @pl.when(t == 0)
    def _():
        s_ref[...] = s0_ref[0]

    for c in range(nc):
        for hh in range(hb):
            hs = slice(hh * HEAD_DIM, (hh + 1) * HEAD_DIM)
            s = s_ref[hh]
            x1 = jnp.dot(wq_ref[0, 2 * c * L:2 * (c + 1) * L, hs], s.astype(BF16), preferred_element_type=F32)
            v_new = u_ref[0, c * L:(c + 1) * L, hs] - x1[:L]
            x2 = jnp.dot(qkk_ref[0, 2 * c * L:2 * (c + 1) * L, hs], v_new.astype(BF16),
                         preferred_element_type=F32)
            s_ref[hh] = s * cd_ref[0, c, 0:1, hs] + x2[L:]
            o = (x1[L:] + x2[:L])[:rows_out]
            z = z_ref[0, c * rows_out:(c + 1) * rows_out, hs]
            o = o * lax.rsqrt(jnp.mean(o * o, axis=-1, keepdims=True) + EPS) * gn_ref[...] * _silu(z)
            o_ref[0, c * rows_out:(c + 1) * rows_out, hs] = o.astype(o_ref.dtype)

    @pl.when(t == pl.num_programs(2) - 1)
    def _():
        so_ref[0] = s_ref[...]


def _delta_scan(u, wq, qkk, cd, proj, z_off, gnorm, s0, *, t_real):
    b, t, hw = u.shape
    L = DELTA_CHUNK
    n_heads = hw // HEAD_DIM
    hb = min(n_heads, 8)
    hbw = hb * HEAD_DIM
    assert z_off % hbw == 0
    nc = min(t // L, 4)
    rows_out = L if t_real == t else t_real
    assert t_real == t or t == L
    nt = t // (nc * L)
    return pl.pallas_call(
        functools.partial(_delta_scan_kernel, hb=hb, nc=nc, rows_out=rows_out),
        out_shape=(jax.ShapeDtypeStruct((b, t_real, hw), BF16), jax.ShapeDtypeStruct(s0.shape, F32)),
        grid=(b, n_heads // hb, nt),
        in_specs=[
            pl.BlockSpec((1, nc * L, hbw), lambda i, h, j: (i, j, h)),
            pl.BlockSpec((1, 2 * nc * L, hbw), lambda i, h, j: (i, j, h)),
            pl.BlockSpec((1, 2 * nc * L, hbw), lambda i, h, j: (i, j, h)),
            pl.BlockSpec((1, nc, SUBLANE, hbw), lambda i, h, j: (i, j, 0, h)),
            pl.BlockSpec((1, nc * rows_out, hbw), lambda i, h, j: (i, j, z_off // hbw + h)),
            pl.BlockSpec((1, HEAD_DIM), lambda i, h, j: (0, 0)),
            pl.BlockSpec((1, hb, HEAD_DIM, HEAD_DIM), lambda i, h, j: (i, h, 0, 0)),
        ],
        out_specs=(
            pl.BlockSpec((1, nc * rows_out, hbw), lambda i, h, j: (i, j, h)),
            pl.BlockSpec((1, hb, HEAD_DIM, HEAD_DIM), lambda i, h, j: (i, h, 0, 0)),
        ),
        scratch_shapes=[pltpu.VMEM((hb, HEAD_DIM, HEAD_DIM), F32)],
        compiler_params=_params("parallel", "parallel", "arbitrary"),
        name="delta_scan",
    )(u, wq, qkk, cd, proj, gnorm.reshape(1, HEAD_DIM), s0)


def _lru_kernel(x_ref, gate_ref, st_ref, h0_ref, cw_ref, cb_ref, wr_ref, wi_ref, br_ref, bi_ref, lam_ref,
                y_ref, nb_ref, hn_ref, xp_ref, hc_ref, *, tt, first, blk):
    t = pl.program_id(2)
    nt = pl.num_programs(2)
    hl = cw_ref.shape[0] - 1

    @pl.when(t == 0)
    def _():
        xp_ref[HALO - hl:HALO, :] = st_ref[0]
        hc_ref[...] = h0_ref[0]

    @pl.when(t > 0)
    def _():
        xp_ref[HALO - hl:HALO, :] = xp_ref[HALO + tt - hl:HALO + tt, :]

    xp_ref[HALO:HALO + tt, :] = x_ref[0]
    xc = _conv_from_scratch(xp_ref, cw_ref, tt) + cb_ref[...]
    r_parts, i_parts = [], []
    for n in range(xc.shape[1] // blk):
        xn = xc[:, n * blk:(n + 1) * blk].astype(BF16)
        r_parts.append(jnp.dot(xn, wr_ref[n], preferred_element_type=F32))
        i_parts.append(jnp.dot(xn, wi_ref[n], preferred_element_type=F32))
    r = jax.nn.sigmoid(jnp.concatenate(r_parts, axis=1) + br_ref[...])
    gi = jax.nn.sigmoid(jnp.concatenate(i_parts, axis=1) + bi_ref[...])
    log_a = -C_LRU * r * _softplus(-lam_ref[...])
    a = jnp.exp(log_a)
    th = jnp.tanh(log_a)
    mult = jnp.sqrt(-2.0 * th / (1.0 - th))
    if first:
        reset = (lax.broadcasted_iota(jnp.int32, a.shape, 0) == 0) & (t == 0)
        a = jnp.where(reset, 0.0, a)
        mult = jnp.where(reset, 1.0, mult)
    u = mult * gi * xc

    grouped = (tt // SUBLANE, SUBLANE, a.shape[1])
    a, u = a.reshape(grouped), u.reshape(grouped)
    in_group = lax.broadcasted_iota(jnp.int32, grouped, 1)
    s = 1
    while s < SUBLANE:
        keep = in_group >= s
        u = a * jnp.where(keep, pltpu.roll(u, s, axis=1), 0.0) + u
        a = a * jnp.where(keep, pltpu.roll(a, s, axis=1), 1.0)
        s *= 2
    a, u = a.reshape(tt, -1), u.reshape(tt, -1)
    carry = hc_ref[...]
    groups = []
    for r in range(tt // SUBLANE):
        rows = slice(r * SUBLANE, (r + 1) * SUBLANE)
        groups.append(u[rows] + a[rows] * carry)
        carry = groups[-1][SUBLANE - 1:SUBLANE, :]
    h = jnp.concatenate(groups, axis=0)
    hc_ref[...] = carry
    y_ref[0] = (h * jax.nn.gelu(gate_ref[0])).astype(y_ref.dtype)

    @pl.when(t == nt - 1)
    def _():
        nb_ref[0] = xp_ref[HALO + tt - hl:HALO + tt, :]
        hn_ref[0] = h[tt - 1:tt, :]


def _lru(proj, x_off, gate_off, st_conv, h0, cw, cb, wr, wi, layer, br, bi, lam, *, first):
    b, t, _ = proj.shape
    _, nblk, blk, _ = wr.shape
    w = nblk * blk
    tt = _tile(t, 256)
    tc = _tile(w, 512)
    assert x_off % tc == 0 and gate_off % tc == 0 and tc % blk == 0
    kw = cw.shape[0]
    col = lambda a: a.reshape(1, w)
    rowspec = pl.BlockSpec((1, tc), lambda i, c, j: (0, c))
    return pl.pallas_call(
        functools.partial(_lru_kernel, tt=tt, first=first, blk=blk),
        out_shape=(
            jax.ShapeDtypeStruct((b, t, w), BF16),
            jax.ShapeDtypeStruct((b, kw - 1, w), F32),
            jax.ShapeDtypeStruct((b, 1, w), F32),
        ),
        grid=(b, w // tc, t // tt),
        in_specs=[
            pl.BlockSpec((1, tt, tc), lambda i, c, j: (i, j, x_off // tc + c)),
            pl.BlockSpec((1, tt, tc), lambda i, c, j: (i, j, gate_off // tc + c)),
            pl.BlockSpec((1, kw - 1, tc), lambda i, c, j: (i, 0, c)),
            pl.BlockSpec((1, 1, tc), lambda i, c, j: (i, 0, c)),
            pl.BlockSpec((kw, tc), lambda i, c, j: (0, c)),
            rowspec,
            pl.BlockSpec((None, tc // blk, blk, blk), lambda i, c, j: (layer, c, 0, 0)),
            pl.BlockSpec((None, tc // blk, blk, blk), lambda i, c, j: (layer, c, 0, 0)),
            rowspec, rowspec, rowspec,
        ],
        out_specs=(
            pl.BlockSpec((1, tt, tc), lambda i, c, j: (i, j, c)),
            pl.BlockSpec((1, kw - 1, tc), lambda i, c, j: (i, 0, c)),
            pl.BlockSpec((1, 1, tc), lambda i, c, j: (i, 0, c)),
        ),
        scratch_shapes=[pltpu.VMEM((HALO + tt, tc), F32), pltpu.VMEM((1, tc), F32)],
        compiler_params=_params("parallel", "parallel", "arbitrary"),
        name="rg_lru",
    )(proj, proj, st_conv, h0.reshape(b, 1, w), cw, col(cb), wr, wi, col(br), col(bi), col(lam))


def _ffn_up_kernel(h_ref, wg_ref, wv_ref, st_ref, cw_ref, cb_ref, act_ref, nb_ref, scr_ref, **tiling):
    def project(h):
        return (jnp.dot(h, wg_ref[...], preferred_element_type=F32),
                jnp.dot(h, wv_ref[...], preferred_element_type=F32))

    def emit(rows, conv, value):
        act_ref[rows, :] = (jax.nn.gelu(conv + cb_ref[...]) * value).astype(act_ref.dtype)

    _conv_proj_rows(h_ref, st_ref, cw_ref, nb_ref, scr_ref, project, emit, **tiling)


def _ffn_up(h, w_up, layer, state, cw, cb, *, t):
    m, d = h.shape
    f = w_up.shape[2] // 2
    b = m // t
    kw = cw.shape[0]
    tn = _tile(f, 512)
    tm, nseg, seg, sub, tiles_per_batch = _conv_proj_tiling(m, t, 1024, 256)
    return pl.pallas_call(
        functools.partial(_ffn_up_kernel, seg=seg, nseg=nseg, tiles_per_batch=tiles_per_batch, sub=sub),
        out_shape=(jax.ShapeDtypeStruct((m, f), BF16), jax.ShapeDtypeStruct((b, kw - 1, f), F32)),
        grid=(f // tn, m // tm),
        in_specs=[
            pl.BlockSpec((tm, d), lambda j, i: (i, 0)),
            pl.BlockSpec((None, d, tn), lambda j, i: (layer, 0, j)),
            pl.BlockSpec((None, d, tn), lambda j, i: (layer, 0, f // tn + j)),
            pl.BlockSpec((nseg, kw - 1, tn), lambda j, i: ((i * tm) // (nseg * t), 0, j)),
            pl.BlockSpec((kw, tn), lambda j, i: (0, j)),
            pl.BlockSpec((1, tn), lambda j, i: (0, j)),
        ],
        out_specs=(
            pl.BlockSpec((tm, tn), lambda j, i: (i, j)),
            pl.BlockSpec((nseg, kw - 1, tn), lambda j, i: ((i * tm) // (nseg * t), 0, j)),
        ),
        scratch_shapes=[pltpu.VMEM((nseg * (HALO + seg), tn), F32)],
        compiler_params=_params("parallel", "arbitrary"),
        name="ffn_up",
    )(h, w_up, w_up, state, cw, cb.reshape(1, f))


def _pad_rows(a, rows):
    return a if a.shape[1] == rows else jnp.pad(a, ((0, 0), (0, rows - a.shape[1]), (0, 0)))


def _trunk(x, mod_all, st_delta, st_qkv, st_lru, st_lruconv, st_ffn, first, wts):
    b, t, d = x.shape
    m = b * t
    n_heads = wts["a_log"].shape[1]
    key_a = n_heads * HEAD_DIM
    qkv_a = 3 * key_a
    w_lru = wts["lru_lambda"].shape[1]
    depth = wts["a_log"].shape[0]
    z_off = 0
    xb_off = z_off + key_a
    gb_off = xb_off + w_lru
    ga_off = gb_off + w_lru
    gbl_off = ga_off + d
    t_pad = -(-t // DELTA_CHUNK) * DELTA_CHUNK
    tm_big = 1024
    outs = {k: [] for k in ("delta", "qkv", "lru", "lruconv", "ffn")}

    for l in range(depth):
        sh1, sc1, gt1, sh2, sc2, gt2 = [c.reshape(b, 1, d) for c in jnp.split(mod_all[l], 6, axis=-1)]
        h = _rms_norm(x, wts["norm1_g"][l], sc1, sh1, out_dtype=BF16).reshape(m, d)
        qkv_c, new_qkv = _qkv_proj(h, wts["w_qkv"], l, st_qkv[l], wts["conv_qkv_w"][l], t=t, key_a=key_a)
        proj = _matmul(h, wts["w_rest"], l, out_dtype=F32, tm=tm_big, tn=1024, tk=d, name="in_proj")
        ab = _matmul(h, wts["w_ab"], l, out_dtype=F32, tm=tm_big, tn=LANE, tk=d, name="in_proj_gates")
        proj3 = proj.reshape(b, t, -1)

        qkv_c = qkv_c.reshape(b, t, qkv_a)
        u, wq, qkk, cd = _delta_prep(_pad_rows(qkv_c, t_pad), _pad_rows(ab.reshape(b, t, LANE), t_pad),
                                     wts["a_log"][l], wts["dt_bias"][l], n_heads=n_heads, t_valid=t)
        o_a, new_delta = _delta_scan(u, wq, qkk, cd, proj3, z_off, wts["gnorm_a"][l], st_delta[l], t_real=t)

        y_lru, new_lruconv, new_lru = _lru(
            proj3, xb_off, gb_off, st_lruconv[l], st_lru[l], wts["conv_lru_w"][l], wts["conv_lru_b"][l],
            wts["lru_wr"], wts["lru_wi"], l, wts["lru_br"][l], wts["lru_bi"][l], wts["lru_lambda"][l],
            first=first)

        mixed = _mixer_merge(o_a.reshape(m, key_a), y_lru.reshape(m, w_lru), wts["w_ao"], wts["w_bo"], l,
                             proj, ga_off, gbl_off)
        tm, tn = _tile(m, 512), _tile(d, 1024)
        x2 = x.reshape(m, d)
        x2 = _matmul(mixed, wts["w_o"], l, out_dtype=F32, tm=tm, tn=tn, tk=d, n_outer=True,
                     epilogue=functools.partial(_residual_epilogue_multi, t=t),
                     extras=[(x2, (tm, tn), lambda i, j: (i, j)), _gate_extra(gt1, t, tm, tn)], name="proj_o")

        h = _rms_norm(x2.reshape(b, t, d), wts["norm2_g"][l], sc2, sh2, out_dtype=BF16).reshape(m, d)
        act, new_ffn = _ffn_up(h, wts["w_up"], l, st_ffn[l], wts["conv_ffn_w"][l], wts["conv_ffn_b"][l], t=t)
        tm, tn = _tile(m, 256), _tile(d, 512)
        x2 = _matmul(act, wts["w_down"], l, out_dtype=F32, tm=tm, tn=tn, tk=act.shape[1], n_outer=True,
                     epilogue=functools.partial(_residual_epilogue_multi, t=t),
                     extras=[(x2, (tm, tn), lambda i, j: (i, j)), _gate_extra(gt2, t, tm, tn)], name="ffn_down")
        x = x2.reshape(b, t, d)

        outs["delta"].append(new_delta)
        outs["qkv"].append(new_qkv)
        outs["lru"].append(new_lru.reshape(b, w_lru))
        outs["lruconv"].append(new_lruconv)
        outs["ffn"].append(new_ffn)

    y = _rms_norm(x, wts["final_g"], out_dtype=F32)
    return (y,) + tuple(jnp.stack(outs[k]) for k in ("delta", "qkv", "lru", "lruconv", "ffn"))


def kernel(x_prompt, x_sample, state_delta, state_qkv_conv, state_lru, state_lru_conv, state_ffn_conv,
           c_prompt, c_sample, norm1_g, w_ada, b_ada, w_in, conv_qkv_w, a_log, dt_bias, gnorm_a, w_ao,
           conv_lru_w, conv_lru_b, lru_wr, lru_br, lru_wi, lru_bi, lru_lambda, w_bo, w_o, norm2_g, w_up,
           conv_ffn_w, conv_ffn_b, w_down, final_g):
    depth, n_heads = a_log.shape
    bp, bs = x_prompt.shape[0], x_sample.shape[0]
    qkv_a = 3 * n_heads * HEAD_DIM
    wts = dict(
        w_qkv=_cast_cols(w_in, 0, qkv_a),
        w_rest=_cast_cols(w_in, qkv_a + 2 * n_heads, w_in.shape[2] - qkv_a - 2 * n_heads),
        w_ab=_cast_cols(w_in, qkv_a, LANE),
        w_ao=w_ao.astype(BF16), w_bo=w_bo.astype(BF16), w_o=w_o.astype(BF16), w_up=w_up.astype(BF16),
        w_down=w_down.astype(BF16), lru_wr=lru_wr.astype(BF16), lru_wi=lru_wi.astype(BF16),
        norm1_g=norm1_g, conv_qkv_w=conv_qkv_w, a_log=a_log, dt_bias=dt_bias, gnorm_a=gnorm_a,
        conv_lru_w=conv_lru_w, conv_lru_b=conv_lru_b, lru_br=lru_br, lru_bi=lru_bi, lru_lambda=lru_lambda,
        norm2_g=norm2_g, conv_ffn_w=conv_ffn_w, conv_ffn_b=conv_ffn_b, final_g=final_g)

    rows = -(-(bp + bs) // 16) * 16
    c_all = jnp.pad(jnp.concatenate([c_prompt, c_sample], axis=0), ((0, rows - bp - bs), (0, 0)))
    mod_all = _ada_mod(c_all, w_ada, b_ada)

    dt = x_prompt.dtype
    zeros = lambda a: jnp.zeros((depth, bp) + a.shape[2:], a.dtype)
    prompt = _trunk(x_prompt, mod_all[:, :bp], zeros(state_delta).astype(F32), zeros(state_qkv_conv),
                    zeros(state_lru).astype(F32), zeros(state_lru_conv), zeros(state_ffn_conv), True, wts)
    sample = _trunk(x_sample, mod_all[:, bp:bp + bs], state_delta, state_qkv_conv, state_lru, state_lru_conv,
                    state_ffn_conv, False, wts)
    cast = lambda outs: tuple(o.astype(dt) for o in outs)
    return (prompt[0], sample[0]) + cast(prompt[1:]) + cast(sample[1:])
```

```python
import functools
import math

import jax
import jax.numpy as jnp
from jax import lax
from jax.experimental import pallas as pl
from jax.experimental.pallas import tpu as pltpu

F32 = jnp.float32
BF16 = jnp.bfloat16

LANE = 128
SUBLANE = 8
VMEM_LIMIT_BYTES = 56 * 1024 * 1024

EPS = 1e-6
C_LRU = 8.0
HEAD_DIM = 128
DELTA_CHUNK = 128
HALO = SUBLANE


def _params(*sem):
    return pltpu.CompilerParams(dimension_semantics=sem, vmem_limit_bytes=VMEM_LIMIT_BYTES)


def _tile(dim, pref):
    if dim <= pref:
        return dim
    for t in range(pref - pref % LANE, 0, -LANE):
        if dim % t == 0:
            return t
    raise ValueError((dim, pref))


def _silu(x):
    return x * jax.nn.sigmoid(x)


def _softplus(x):
    return jnp.maximum(x, 0.0) + jnp.log1p(jnp.exp(-jnp.abs(x)))


def _bdot(a, b):
    return jnp.dot(a.astype(BF16), b.astype(BF16), preferred_element_type=F32)


def _ada_kernel(c_ref, w_ref, b_ref, o_ref):
    o_ref[0] = _bdot(_silu(c_ref[...]), w_ref[0]) + b_ref[0]


def _ada_mod(c_all, w_ada, b_ada):
    rows, d = c_all.shape
    depth, _, n = w_ada.shape
    tn = _tile(n, 512)
    return pl.pallas_call(
        _ada_kernel,
        out_shape=jax.ShapeDtypeStruct((depth, rows, n), F32),
        grid=(depth, n // tn),
        in_specs=[
            pl.BlockSpec((rows, d), lambda l, j: (0, 0)),
            pl.BlockSpec((1, d, tn), lambda l, j: (l, 0, j)),
            pl.BlockSpec((1, 1, tn), lambda l, j: (l, 0, j)),
        ],
        out_specs=pl.BlockSpec((1, rows, tn), lambda l, j: (l, 0, j)),
        compiler_params=_params("parallel", "parallel"),
        name="ada_mod",
    )(c_all, w_ada, b_ada.reshape(depth, 1, n))


def _norm_kernel(*refs, modulate):
    if modulate:
        x_ref, g_ref, sc_ref, sh_ref, o_ref = refs
    else:
        x_ref, g_ref, o_ref = refs
    x = x_ref[0]
    y = x * lax.rsqrt(jnp.mean(x * x, axis=-1, keepdims=True) + EPS)
    if modulate:
        y = y * (g_ref[...] * (1.0 + sc_ref[0])) + sh_ref[0]
    else:
        y = y * g_ref[...]
    o_ref[0] = y.astype(o_ref.dtype)


def _rms_norm(x, g, sc=None, sh=None, *, out_dtype):
    b, t, d = x.shape
    tt = _tile(t, 512)
    modulate = sc is not None
    row = pl.BlockSpec((1, 1, d), lambda i, j: (i, 0, 0))
    in_specs = [pl.BlockSpec((1, tt, d), lambda i, j: (i, j, 0)), pl.BlockSpec((1, d), lambda i, j: (0, 0))]
    args = [x, g.reshape(1, d)]
    if modulate:
        in_specs += [row, row]
        args += [sc, sh]
    return pl.pallas_call(
        functools.partial(_norm_kernel, modulate=modulate),
        out_shape=jax.ShapeDtypeStruct((b, t, d), out_dtype),
        grid=(b, t // tt),
        in_specs=in_specs,
        out_specs=pl.BlockSpec((1, tt, d), lambda i, j: (i, j, 0)),
        compiler_params=_params("parallel", "parallel"),
        name="rms_norm_mod" if modulate else "rms_norm",
    )(*args)


def _matmul_kernel(*refs, n_extra, nk, epilogue):
    a_ref, b_ref = refs[:2]
    extra = refs[2:2 + n_extra]
    o_ref = refs[2 + n_extra]

    def finish(acc):
        o_ref[...] = epilogue(acc, *[e[...] for e in extra]).astype(o_ref.dtype)

    part = jnp.dot(a_ref[...], b_ref[...], preferred_element_type=F32)
    if nk == 1:
        finish(part)
        return
    acc_ref = refs[3 + n_extra]
    k = pl.program_id(2)

    @pl.when(k == 0)
    def _():
        acc_ref[...] = part

    @pl.when(k > 0)
    def _():
        acc_ref[...] += part

    @pl.when(k == nk - 1)
    def _():
        finish(acc_ref[...])


def _matmul(a, b, layer, *, out_dtype, tm, tn, tk, epilogue=None, extras=(), n_outer=False, name):
    m, k = a.shape
    _, _, n = b.shape
    tm, tn, tk = _tile(m, tm), _tile(n, tn), _tile(k, tk)
    nk = k // tk
    if epilogue is None:
        epilogue = lambda acc: acc
    if n_outer:
        grid = (n // tn, m // tm, nk)
        order = lambda f: (lambda j, i, kk: f(i, j, kk))
    else:
        grid = (m // tm, n // tn, nk)
        order = lambda f: f
    in_specs = [pl.BlockSpec((tm, tk), order(lambda i, j, kk: (i, kk))),
                pl.BlockSpec((None, tk, tn), order(lambda i, j, kk: (layer, kk, j)))]
    for _, shape, imap in extras:
        in_specs.append(pl.BlockSpec(shape, order(functools.partial(lambda i, j, kk, f: f(i, j), f=imap))))
    return pl.pallas_call(
        functools.partial(_matmul_kernel, n_extra=len(extras), nk=nk, epilogue=epilogue),
        out_shape=jax.ShapeDtypeStruct((m, n), out_dtype),
        grid=grid,
        in_specs=in_specs,
        out_specs=pl.BlockSpec((tm, tn), order(lambda i, j, kk: (i, j))),
        scratch_shapes=[pltpu.VMEM((tm, tn), F32)] if nk > 1 else [],
        compiler_params=_params("parallel", "parallel", "arbitrary"),
        name=name,
    )(a, b, *[e[0] for e in extras])


def _mixer_merge_kernel(oa_ref, ob_ref, wa_ref, wb_ref, ga_ref, gb_ref, o_ref):
    y_a = jnp.dot(oa_ref[...], wa_ref[...], preferred_element_type=F32)
    y_b = jnp.dot(ob_ref[...], wb_ref[...], preferred_element_type=F32)
    o_ref[...] = (jax.nn.sigmoid(ga_ref[...]) * y_a + jax.nn.sigmoid(gb_ref[...]) * y_b).astype(o_ref.dtype)


def _mixer_merge(o_a, o_b, w_ao, w_bo, layer, proj, ga_off, gb_off):
    m, ka = o_a.shape
    _, kb = o_b.shape
    n = w_ao.shape[2]
    tm, tn = _tile(m, 512), _tile(math.gcd(n, ga_off, gb_off), 1024)
    return pl.pallas_call(
        _mixer_merge_kernel,
        out_shape=jax.ShapeDtypeStruct((m, n), BF16),
        grid=(n // tn, m // tm),
        in_specs=[
            pl.BlockSpec((tm, ka), lambda j, i: (i, 0)),
            pl.BlockSpec((tm, kb), lambda j, i: (i, 0)),
            pl.BlockSpec((None, ka, tn), lambda j, i: (layer, 0, j)),
            pl.BlockSpec((None, kb, tn), lambda j, i: (layer, 0, j)),
            pl.BlockSpec((tm, tn), lambda j, i: (i, ga_off // tn + j)),
            pl.BlockSpec((tm, tn), lambda j, i: (i, gb_off // tn + j)),
        ],
        out_specs=pl.BlockSpec((tm, tn), lambda j, i: (i, j)),
        compiler_params=_params("parallel", "parallel"),
        name="mixer_merge",
    )(o_a, o_b, w_ao, w_bo, proj, proj)


def _gate_extra(gate, t, tm, tn):
    if tm <= t:
        return (gate, (1, 1, tn), lambda i, j: ((i * tm) // t, 0, j))
    assert tm % t == 0
    return (gate, (tm // t, 1, tn), lambda i, j: (i, 0, j))


def _residual_epilogue_multi(acc, x, gate, *, t):
    nb = gate.shape[0]
    if nb == 1:
        return x + gate[0] * acc
    parts = [x[s * t:(s + 1) * t] + gate[s] * acc[s * t:(s + 1) * t] for s in range(nb)]
    return jnp.concatenate(parts, axis=0)


def _conv_from_scratch(xp_ref, w_ref, rows, base=0):
    kw = w_ref.shape[0]
    y = None
    for j in range(kw):
        term = w_ref[j:j + 1, :] * xp_ref[pl.ds(base + HALO - (kw - 1) + j, rows), :]
        y = term if y is None else y + term
    return y


def _conv_proj_rows(h_ref, st_ref, cw_ref, nb_ref, scr_ref, project, emit, *, seg, nseg, tiles_per_batch, sub):
    i = pl.program_id(1)
    hl = cw_ref.shape[0] - 1
    pos = i % tiles_per_batch
    if nseg == 1:
        if tiles_per_batch == 1:
            scr_ref[HALO - hl:HALO, :] = st_ref[0]
        else:
            @pl.when(pos == 0)
            def _():
                scr_ref[HALO - hl:HALO, :] = st_ref[0]

            @pl.when(pos > 0)
            def _():
                scr_ref[HALO - hl:HALO, :] = scr_ref[HALO + seg - hl:HALO + seg, :]

        def start(r):
            x, aux = project(h_ref[r * sub:(r + 1) * sub, :])
            scr_ref[HALO + r * sub:HALO + (r + 1) * sub, :] = x
            return aux

        n_sub = seg // sub
        aux = start(0)
        for r in range(n_sub):
            aux_next = start(r + 1) if r + 1 < n_sub else None
            emit(slice(r * sub, (r + 1) * sub), _conv_from_scratch(scr_ref, cw_ref, sub, r * sub), aux)
            aux = aux_next

        if tiles_per_batch == 1:
            nb_ref[0] = scr_ref[HALO + seg - hl:HALO + seg, :]
        else:
            @pl.when(pos == tiles_per_batch - 1)
            def _():
                nb_ref[0] = scr_ref[HALO + seg - hl:HALO + seg, :]
        return

    x, aux = project(h_ref[...])
    for s in range(nseg):
        base = s * (HALO + seg)
        rows = slice(s * seg, (s + 1) * seg)
        scr_ref[base + HALO - hl:base + HALO, :] = st_ref[s]
        scr_ref[base + HALO:base + HALO + seg, :] = x[rows]
        emit(rows, _conv_from_scratch(scr_ref, cw_ref, seg, base), None if aux is None else aux[rows])
        nb_ref[s] = scr_ref[base + HALO + seg - hl:base + HALO + seg, :]


def _conv_proj_tiling(m, t, tm_pref, sub_pref):
    if t >= 512:
        tm, nseg = _tile(t, tm_pref), 1
    else:
        tm = _tile(m, max(t, 128))
        nseg = tm // t
    seg = tm // nseg
    tiles_per_batch = max(t // tm, 1)
    assert nseg == 1 or tiles_per_batch == 1
    return tm, nseg, seg, _tile(seg, sub_pref), tiles_per_batch


def _qkv_proj_kernel(h_ref, w_ref, st_ref, cw_ref, o_ref, nb_ref, scr_ref, *, n_qk_tiles, q_tiles, **tiling):
    j = pl.program_id(0)
    is_qk = j < n_qk_tiles
    scale = jnp.where(j < q_tiles, HEAD_DIM ** -0.5, 1.0).astype(F32)

    def project(h):
        return jnp.dot(h, w_ref[...], preferred_element_type=F32), None

    def emit(rows, conv, _):
        y = _silu(conv)
        for hh in range(y.shape[1] // HEAD_DIM):
            hs = slice(hh * HEAD_DIM, (hh + 1) * HEAD_DIM)
            blk = y[:, hs]
            inv = lax.rsqrt(jnp.sum(blk * blk, axis=-1, keepdims=True) + EPS) * scale
            o_ref[rows, hs] = blk * jnp.where(is_qk, inv, 1.0)

    _conv_proj_rows(h_ref, st_ref, cw_ref, nb_ref, scr_ref, project, emit, **tiling)


def _qkv_proj(h, w_qkv, layer, state, cw, *, t, key_a):
    m, d = h.shape
    n = w_qkv.shape[2]
    b = m // t
    kw = cw.shape[0]
    tn = _tile(key_a, 512)
    tm, nseg, seg, sub, tiles_per_batch = _conv_proj_tiling(m, t, 1024, 128)
    state_spec = pl.BlockSpec((nseg, kw - 1, tn), lambda j, i: ((i * tm) // (nseg * t), 0, j))
    return pl.pallas_call(
        functools.partial(_qkv_proj_kernel, n_qk_tiles=2 * key_a // tn, q_tiles=key_a // tn, seg=seg, nseg=nseg,
                          tiles_per_batch=tiles_per_batch, sub=sub),
        out_shape=(jax.ShapeDtypeStruct((m, n), F32), jax.ShapeDtypeStruct((b, kw - 1, n), F32)),
        grid=(n // tn, m // tm),
        in_specs=[
            pl.BlockSpec((tm, d), lambda j, i: (i, 0)),
            pl.BlockSpec((None, d, tn), lambda j, i: (layer, 0, j)),
            state_spec,
            pl.BlockSpec((kw, tn), lambda j, i: (0, j)),
        ],
        out_specs=(pl.BlockSpec((tm, tn), lambda j, i: (i, j)), state_spec),
        scratch_shapes=[pltpu.VMEM((nseg * (HALO + seg), tn), F32)],
        compiler_params=_params("parallel", "arbitrary"),
        name="qkv_proj",
    )(h, w_qkv, state, cw)


def _block_diag(m):
    z = jnp.zeros((m.shape[0], HEAD_DIM), m.dtype)
    return jnp.concatenate([jnp.concatenate([m[:, :HEAD_DIM], z], axis=1),
                            jnp.concatenate([z, m[:, HEAD_DIM:]], axis=1)], axis=0)


def _delta_prep_kernel(q_ref, k_ref, v_ref, ab_ref, alog_ref, dtb_ref, lvl_ref, u_ref, wq_ref, qkk_ref, cd_ref,
                       t_ref, a_ref, rhs_ref, *, n_heads, t_valid):
    L, D, W = DELTA_CHUNK, HEAD_DIM, 2 * HEAD_DIM
    n_pairs = n_heads // 2
    n_levels = lvl_ref.shape[0]
    t = pl.program_id(1)
    ab = ab_ref[0]
    valid = (t * L + lax.broadcasted_iota(jnp.int32, ab.shape, 0)) < t_valid
    g = jnp.where(valid, -jnp.exp(alog_ref[...]) * _softplus(ab + dtb_ref[...]), 0.0)
    beta = jnp.where(valid, jax.nn.sigmoid(ab), 0.0)
    lane = lax.broadcasted_iota(jnp.int32, ab.shape, 1)
    tri = (lax.broadcasted_iota(jnp.int32, (L, L), 0) >= lax.broadcasted_iota(jnp.int32, (L, L), 1)).astype(F32)
    cum = jnp.dot(tri, g, precision=lax.Precision.HIGHEST, preferred_element_type=F32)
    cum_t = cum.T
    ii = lax.broadcasted_iota(jnp.int32, (L, W), 0)
    jj = lax.broadcasted_iota(jnp.int32, (L, W), 1) & (D - 1)
    causal = ii >= jj
    strict = ii > jj
    eye = jnp.where(ii == jj, 1.0, 0.0)
    second = lax.broadcasted_iota(jnp.int32, (1, W), 1) >= D

    def pair_cols(f):
        return jnp.where(second, f(1), f(0))

    for p in range(n_pairs):
        cols = slice(p * W, (p + 1) * W)
        pick = lambda x, idx: jnp.sum(jnp.where(lane == idx, x, 0.0), axis=1, keepdims=True)
        cum_col = pair_cols(lambda e: pick(cum, 2 * p + e))
        b_col = pair_cols(lambda e: pick(beta, n_heads + 2 * p + e))
        cum_row = jnp.concatenate([cum_t[2 * p:2 * p + 1, :], cum_t[2 * p + 1:2 * p + 2, :]], axis=1)
        g_last = cum_col[L - 1:L, :]
        gamma = jnp.where(causal, jnp.exp(jnp.where(causal, cum_col - cum_row, 0.0)), 0.0)
        e_col = jnp.exp(cum_col)

        q, k, v = q_ref[0, :, cols], k_ref[0, :, cols], v_ref[0, :, cols]
        kb = k * b_col
        kq = lax.dot_general(jnp.concatenate([kb, q], axis=0).astype(BF16), _block_diag(k.astype(BF16)),
                             (((1,), (1,)), ((), ())), preferred_element_type=F32)
        a_mat = jnp.where(strict, kq[:L] * gamma, 0.0)
        a_ref[p] = a_mat.astype(BF16)
        t_ref[p] = eye - jnp.where((ii ^ jj) == 1, a_mat, 0.0)
        vb, kbe = v * b_col, kb * e_col
        rhs_ref[p] = jnp.concatenate([vb[:, :D], kbe[:, :D], vb[:, D:], kbe[:, D:]], axis=1)
        wq_ref[0, L:2 * L, cols] = (q * e_col).astype(BF16)
        qkk_ref[0, 0:L, cols] = (kq[L:] * gamma).astype(BF16)
        kd = k * jnp.exp(g_last - cum_col)
        qkk_ref[0, L:2 * L, cols] = jnp.concatenate([kd[:, :D].T, kd[:, D:].T], axis=1).astype(BF16)
        cd_ref[0, 0, :, cols] = jnp.broadcast_to(jnp.exp(g_last), (SUBLANE, W))

    for s in range(1, n_levels):
        for p in range(n_pairs):
            tm = t_ref[p]
            tb = tm.astype(BF16)
            pm = jnp.dot(tb, _block_diag(a_ref[p] * lvl_ref[s]), preferred_element_type=F32)
            qm = jnp.dot(pm.astype(BF16), _block_diag(tb), preferred_element_type=F32)
            t_ref[p] = tm - qm

    for p in range(n_pairs):
        xb = (t_ref[p] - eye).astype(BF16)
        for e in range(2):
            hs = slice((2 * p + e) * D, (2 * p + e + 1) * D)
            rhs = rhs_ref[p, :, e * W:(e + 1) * W]
            sol = rhs + jnp.dot(xb[:, e * D:(e + 1) * D], rhs.astype(BF16), preferred_element_type=F32)
            u_ref[0, :, hs] = sol[:, :D]
            wq_ref[0, 0:L, hs] = sol[:, D:].astype(BF16)


def _level_masks():
    L = DELTA_CHUNK
    ii = jnp.arange(L)[:, None]
    jj = jnp.arange(2 * HEAD_DIM)[None, :] % HEAD_DIM
    return jnp.stack([(((ii >> s) ^ (jj >> s)) == 1) for s in range(L.bit_length() - 1)]).astype(BF16)


def _delta_prep(qkv, ab, a_log, dt_bias, *, n_heads, t_valid):
    b, t, _ = qkv.shape
    L = DELTA_CHUNK
    assert L == HEAD_DIM and n_heads % 2 == 0
    hw = n_heads * HEAD_DIM
    nc = t // L
    lvl = _level_masks()
    pad = lambda a: jnp.pad(a.reshape(1, n_heads), ((0, 0), (0, LANE - n_heads)))
    tok = lambda off: pl.BlockSpec((1, L, hw), lambda i, j: (i, j, off))
    row = pl.BlockSpec((1, LANE), lambda i, j: (0, 0))
    pair_scratch = lambda width, dtype=F32: pltpu.VMEM((n_heads // 2, L, width), dtype)
    return pl.pallas_call(
        functools.partial(_delta_prep_kernel, n_heads=n_heads, t_valid=t_valid),
        out_shape=(
            jax.ShapeDtypeStruct((b, t, hw), F32),
            jax.ShapeDtypeStruct((b, 2 * t, hw), BF16),
            jax.ShapeDtypeStruct((b, 2 * t, hw), BF16),
            jax.ShapeDtypeStruct((b, nc, SUBLANE, hw), F32),
        ),
        grid=(b, nc),
        in_specs=[tok(0), tok(1), tok(2), pl.BlockSpec((1, L, LANE), lambda i, j: (i, j, 0)), row, row,
                  pl.BlockSpec(lvl.shape, lambda i, j: (0, 0, 0))],
        out_specs=(
            pl.BlockSpec((1, L, hw), lambda i, j: (i, j, 0)),
            pl.BlockSpec((1, 2 * L, hw), lambda i, j: (i, j, 0)),
            pl.BlockSpec((1, 2 * L, hw), lambda i, j: (i, j, 0)),
            pl.BlockSpec((1, 1, SUBLANE, hw), lambda i, j: (i, j, 0, 0)),
        ),
        scratch_shapes=[pair_scratch(2 * HEAD_DIM), pair_scratch(2 * HEAD_DIM, BF16), pair_scratch(4 * HEAD_DIM)],
        compiler_params=_params("parallel", "parallel"),
        name="delta_prep",
    )(qkv, qkv, qkv, ab, pad(a_log), pad(dt_bias), lvl)


def _delta_scan_kernel(u_ref, wq_ref, qkk_ref, cd_ref, z_ref, gn_ref, s0_ref, o_ref, so_ref, s_ref,
                       *, hb, nc, rows_out):
    L = DELTA_CHUNK
    t = pl.program_id(2)

    @pl.when(t == 0)
    def _():
        s_ref[...] = s0_ref[0]

    for c in range(nc):
        for hh in range(hb):
            hs = slice(hh * HEAD_DIM, (hh + 1) * HEAD_DIM)
            s = s_ref[hh]
            x1 = jnp.dot(wq_ref[0, 2 * c * L:2 * (c + 1) * L, hs], s.astype(BF16), preferred_element_type=F32)
            v_new = u_ref[0, c * L:(c + 1) * L, hs] - x1[:L]
            x2 = jnp.dot(qkk_ref[0, 2 * c * L:2 * (c + 1) * L, hs], v_new.astype(BF16),
                         preferred_element_type=F32)
            s_ref[hh] = s * cd_ref[0, c, 0:1, hs] + x2[L:]
            o = (x1[L:] + x2[:L])[:rows_out]
            z = z_ref[0, c * rows_out:(c + 1) * rows_out, hs]
            o = o * lax.rsqrt(jnp.mean(o * o, axis=-1, keepdims=True) + EPS) * gn_ref[...] * _silu(z)
            o_ref[0, c * rows_out:(c + 1) * rows_out, hs] = o.astype(o_ref.dtype)

    @pl.when(t == pl.num_programs(2) - 1)
    def _():
        so_ref[0] = s_ref[...]


def _delta_scan(u, wq, qkk, cd, proj, z_off, gnorm, s0, *, t_real):
    b, t, hw = u.shape
    L = DELTA_CHUNK
    n_heads = hw // HEAD_DIM
    hb = min(n_heads, 8)
    hbw = hb * HEAD_DIM
    assert z_off % hbw == 0
    nc = min(t // L, 4)
    rows_out = L if t_real == t else t_real
    assert t_real == t or t == L
    nt = t // (nc * L)
    return pl.pallas_call(
        functools.partial(_delta_scan_kernel, hb=hb, nc=nc, rows_out=rows_out),
        out_shape=(jax.ShapeDtypeStruct((b, t_real, hw), BF16), jax.ShapeDtypeStruct(s0.shape, F32)),
        grid=(b, n_heads // hb, nt),
        in_specs=[
            pl.BlockSpec((1, nc * L, hbw), lambda i, h, j: (i, j, h)),
            pl.BlockSpec((1, 2 * nc * L, hbw), lambda i, h, j: (i, j, h)),
            pl.BlockSpec((1, 2 * nc * L, hbw), lambda i, h, j: (i, j, h)),
            pl.BlockSpec((1, nc, SUBLANE, hbw), lambda i, h, j: (i, j, 0, h)),
            pl.BlockSpec((1, nc * rows_out, hbw), lambda i, h, j: (i, j, z_off // hbw + h)),
            pl.BlockSpec((1, HEAD_DIM), lambda i, h, j: (0, 0)),
            pl.BlockSpec((1, hb, HEAD_DIM, HEAD_DIM), lambda i, h, j: (i, h, 0, 0)),
        ],
        out_specs=(
            pl.BlockSpec((1, nc * rows_out, hbw), lambda i, h, j: (i, j, h)),
            pl.BlockSpec((1, hb, HEAD_DIM, HEAD_DIM), lambda i, h, j: (i, h, 0, 0)),
        ),
        scratch_shapes=[pltpu.VMEM((hb, HEAD_DIM, HEAD_DIM), F32)],
        compiler_params=_params("parallel", "parallel", "arbitrary"),
        name="delta_scan",
    )(u, wq, qkk, cd, proj, gnorm.reshape(1, HEAD_DIM), s0)


def _lru_kernel(x_ref, gate_ref, st_ref, h0_ref, cw_ref, cb_ref, wr_ref, wi_ref, br_ref, bi_ref, lam_ref,
                y_ref, nb_ref, hn_ref, xp_ref, hc_ref, *, tt, first, blk):
    t = pl.program_id(2)
    nt = pl.num_programs(2)
    hl = cw_ref.shape[0] - 1

    @pl.when(t == 0)
    def _():
        xp_ref[HALO - hl:HALO, :] = st_ref[0]
        hc_ref[...] = h0_ref[0]

    @pl.when(t > 0)
    def _():
        xp_ref[HALO - hl:HALO, :] = xp_ref[HALO + tt - hl:HALO + tt, :]

    xp_ref[HALO:HALO + tt, :] = x_ref[0]
    xc = _conv_from_scratch(xp_ref, cw_ref, tt) + cb_ref[...]
    r_parts, i_parts = [], []
    for n in range(xc.shape[1] // blk):
        xn = xc[:, n * blk:(n + 1) * blk].astype(BF16)
        r_parts.append(jnp.dot(xn, wr_ref[n], preferred_element_type=F32))
        i_parts.append(jnp.dot(xn, wi_ref[n], preferred_element_type=F32))
    r = jax.nn.sigmoid(jnp.concatenate(r_parts, axis=1) + br_ref[...])
    gi = jax.nn.sigmoid(jnp.concatenate(i_parts, axis=1) + bi_ref[...])
    log_a = -C_LRU * r * _softplus(-lam_ref[...])
    a = jnp.exp(log_a)
    th = jnp.tanh(log_a)
    mult = jnp.sqrt(-2.0 * th / (1.0 - th))
    if first:
        reset = (lax.broadcasted_iota(jnp.int32, a.shape, 0) == 0) & (t == 0)
        a = jnp.where(reset, 0.0, a)
        mult = jnp.where(reset, 1.0, mult)
    u = mult * gi * xc

    grouped = (tt // SUBLANE, SUBLANE, a.shape[1])
    a, u = a.reshape(grouped), u.reshape(grouped)
    in_group = lax.broadcasted_iota(jnp.int32, grouped, 1)
    s = 1
    while s < SUBLANE:
        keep = in_group >= s
        u = a * jnp.where(keep, pltpu.roll(u, s, axis=1), 0.0) + u
        a = a * jnp.where(keep, pltpu.roll(a, s, axis=1), 1.0)
        s *= 2
    a, u = a.reshape(tt, -1), u.reshape(tt, -1)
    carry = hc_ref[...]
    groups = []
    for r in range(tt // SUBLANE):
        rows = slice(r * SUBLANE, (r + 1) * SUBLANE)
        groups.append(u[rows] + a[rows] * carry)
        carry = groups[-1][SUBLANE - 1:SUBLANE, :]
    h = jnp.concatenate(groups, axis=0)
    hc_ref[...] = carry
    y_ref[0] = (h * jax.nn.gelu(gate_ref[0])).astype(y_ref.dtype)

    @pl.when(t == nt - 1)
    def _():
        nb_ref[0] = xp_ref[HALO + tt - hl:HALO + tt, :]
        hn_ref[0] = h[tt - 1:tt, :]


def _lru(proj, x_off, gate_off, st_conv, h0, cw, cb, wr, wi, layer, br, bi, lam, *, first):
    b, t, _ = proj.shape
    _, nblk, blk, _ = wr.shape
    w = nblk * blk
    tt = _tile(t, 256)
    tc = _tile(w, 512)
    assert x_off % tc == 0 and gate_off % tc == 0 and tc % blk == 0
    kw = cw.shape[0]
    col = lambda a: a.reshape(1, w)
    rowspec = pl.BlockSpec((1, tc), lambda i, c, j: (0, c))
    return pl.pallas_call(
        functools.partial(_lru_kernel, tt=tt, first=first, blk=blk),
        out_shape=(
            jax.ShapeDtypeStruct((b, t, w), BF16),
            jax.ShapeDtypeStruct((b, kw - 1, w), F32),
            jax.ShapeDtypeStruct((b, 1, w), F32),
        ),
        grid=(b, w // tc, t // tt),
        in_specs=[
            pl.BlockSpec((1, tt, tc), lambda i, c, j: (i, j, x_off // tc + c)),
            pl.BlockSpec((1, tt, tc), lambda i, c, j: (i, j, gate_off // tc + c)),
            pl.BlockSpec((1, kw - 1, tc), lambda i, c, j: (i, 0, c)),
            pl.BlockSpec((1, 1, tc), lambda i, c, j: (i, 0, c)),
            pl.BlockSpec((kw, tc), lambda i, c, j: (0, c)),
            rowspec,
            pl.BlockSpec((None, tc // blk, blk, blk), lambda i, c, j: (layer, c, 0, 0)),
            pl.BlockSpec((None, tc // blk, blk, blk), lambda i, c, j: (layer, c, 0, 0)),
            rowspec, rowspec, rowspec,
        ],
        out_specs=(
            pl.BlockSpec((1, tt, tc), lambda i, c, j: (i, j, c)),
            pl.BlockSpec((1, kw - 1, tc), lambda i, c, j: (i, 0, c)),
            pl.BlockSpec((1, 1, tc), lambda i, c, j: (i, 0, c)),
        ),
        scratch_shapes=[pltpu.VMEM((HALO + tt, tc), F32), pltpu.VMEM((1, tc), F32)],
        compiler_params=_params("parallel", "parallel", "arbitrary"),
        name="rg_lru",
    )(proj, proj, st_conv, h0.reshape(b, 1, w), cw, col(cb), wr, wi, col(br), col(bi), col(lam))


def _ffn_up_kernel(h_ref, wg_ref, wv_ref, st_ref, cw_ref, cb_ref, act_ref, nb_ref, scr_ref, **tiling):
    def project(h):
        return (jnp.dot(h, wg_ref[...], preferred_element_type=F32),
                jnp.dot(h, wv_ref[...], preferred_element_type=F32))

    def emit(rows, conv, value):
        act_ref[rows, :] = (jax.nn.gelu(conv + cb_ref[...]) * value).astype(act_ref.dtype)

    _conv_proj_rows(h_ref, st_ref, cw_ref, nb_ref, scr_ref, project, emit, **tiling)


def _ffn_up(h, w_up, layer, state, cw, cb, *, t):
    m, d = h.shape
    f = w_up.shape[2] // 2
    b = m // t
    kw = cw.shape[0]
    tn = _tile(f, 512)
    tm, nseg, seg, sub, tiles_per_batch = _conv_proj_tiling(m, t, 1024, 256)
    return pl.pallas_call(
        functools.partial(_ffn_up_kernel, seg=seg, nseg=nseg, tiles_per_batch=tiles_per_batch, sub=sub),
        out_shape=(jax.ShapeDtypeStruct((m, f), BF16), jax.ShapeDtypeStruct((b, kw - 1, f), F32)),
        grid=(f // tn, m // tm),
        in_specs=[
            pl.BlockSpec((tm, d), lambda j, i: (i, 0)),
            pl.BlockSpec((None, d, tn), lambda j, i: (layer, 0, j)),
            pl.BlockSpec((None, d, tn), lambda j, i: (layer, 0, f // tn + j)),
            pl.BlockSpec((nseg, kw - 1, tn), lambda j, i: ((i * tm) // (nseg * t), 0, j)),
            pl.BlockSpec((kw, tn), lambda j, i: (0, j)),
            pl.BlockSpec((1, tn), lambda j, i: (0, j)),
        ],
        out_specs=(
            pl.BlockSpec((tm, tn), lambda j, i: (i, j)),
            pl.BlockSpec((nseg, kw - 1, tn), lambda j, i: ((i * tm) // (nseg * t), 0, j)),
        ),
        scratch_shapes=[pltpu.VMEM((nseg * (HALO + seg), tn), F32)],
        compiler_params=_params("parallel", "arbitrary"),
        name="ffn_up",
    )(h, w_up, w_up, state, cw, cb.reshape(1, f))


def _pad_rows(a, rows):
    return a if a.shape[1] == rows else jnp.pad(a, ((0, 0), (0, rows - a.shape[1]), (0, 0)))


def _trunk(x, mod_all, st_delta, st_qkv, st_lru, st_lruconv, st_ffn, first, wts):
    b, t, d = x.shape
    m = b * t
    n_heads = wts["a_log"].shape[1]
    key_a = n_heads * HEAD_DIM
    qkv_a = 3 * key_a
    w_lru = wts["lru_lambda"].shape[1]
    depth = wts["a_log"].shape[0]
    z_off = 0
    xb_off = z_off + key_a
    gb_off = xb_off + w_lru
    ga_off = gb_off + w_lru
    gbl_off = ga_off + d
    t_pad = -(-t // DELTA_CHUNK) * DELTA_CHUNK
    tm_big = 1024
    outs = {k: [] for k in ("delta", "qkv", "lru", "lruconv", "ffn")}

    for l in range(depth):
        sh1, sc1, gt1, sh2, sc2, gt2 = [c.reshape(b, 1, d) for c in jnp.split(mod_all[l], 6, axis=-1)]
        h = _rms_norm(x, wts["norm1_g"][l], sc1, sh1, out_dtype=BF16).reshape(m, d)
        qkv_c, new_qkv = _qkv_proj(h, wts["w_qkv"], l, st_qkv[l], wts["conv_qkv_w"][l], t=t, key_a=key_a)
        proj = _matmul(h, wts["w_rest"], l, out_dtype=F32, tm=tm_big, tn=1024, tk=d, name="in_proj")
        ab = _matmul(h, wts["w_ab"], l, out_dtype=F32, tm=tm_big, tn=LANE, tk=d, name="in_proj_gates")
        proj3 = proj.reshape(b, t, -1)

        qkv_c = qkv_c.reshape(b, t, qkv_a)
        u, wq, qkk, cd = _delta_prep(_pad_rows(qkv_c, t_pad), _pad_rows(ab.reshape(b, t, LANE), t_pad),
                                     wts["a_log"][l], wts["dt_bias"][l], n_heads=n_heads, t_valid=t)
        o_a, new_delta = _delta_scan(u, wq, qkk, cd, proj3, z_off, wts["gnorm_a"][l], st_delta[l], t_real=t)

        y_lru, new_lruconv, new_lru = _lru(
            proj3, xb_off, gb_off, st_lruconv[l], st_lru[l], wts["conv_lru_w"][l], wts["conv_lru_b"][l],
            wts["lru_wr"], wts["lru_wi"], l, wts["lru_br"][l], wts["lru_bi"][l], wts["lru_lambda"][l],
            first=first)

        mixed = _mixer_merge(o_a.reshape(m, key_a), y_lru.reshape(m, w_lru), wts["w_ao"], wts["w_bo"], l,
                             proj, ga_off, gbl_off)
        tm, tn = _tile(m, 512), _tile(d, 1024)
        x2 = x.reshape(m, d)
        x2 = _matmul(mixed, wts["w_o"], l, out_dtype=F32, tm=tm, tn=tn, tk=d, n_outer=True,
                     epilogue=functools.partial(_residual_epilogue_multi, t=t),
                     extras=[(x2, (tm, tn), lambda i, j: (i, j)), _gate_extra(gt1, t, tm, tn)], name="proj_o")

        h = _rms_norm(x2.reshape(b, t, d), wts["norm2_g"][l], sc2, sh2, out_dtype=BF16).reshape(m, d)
        act, new_ffn = _ffn_up(h, wts["w_up"], l, st_ffn[l], wts["conv_ffn_w"][l], wts["conv_ffn_b"][l], t=t)
        tm, tn = _tile(m, 512), _tile(d, 512)
        x2 = _matmul(act, wts["w_down"], l, out_dtype=F32, tm=tm, tn=tn, tk=act.shape[1], n_outer=True,
                     epilogue=functools.partial(_residual_epilogue_multi, t=t),
                     extras=[(x2, (tm, tn), lambda i, j: (i, j)), _gate_extra(gt2, t, tm, tn)], name="ffn_down")
        x = x2.reshape(b, t, d)

        outs["delta"].append(new_delta)
        outs["qkv"].append(new_qkv)
        outs["lru"].append(new_lru.reshape(b, w_lru))
        outs["lruconv"].append(new_lruconv)
        outs["ffn"].append(new_ffn)

    y = _rms_norm(x, wts["final_g"], out_dtype=F32)
    return (y,) + tuple(jnp.stack(outs[k]) for k in ("delta", "qkv", "lru", "lruconv", "ffn"))


def kernel(x_prompt, x_sample, state_delta, state_qkv_conv, state_lru, state_lru_conv, state_ffn_conv,
           c_prompt, c_sample, norm1_g, w_ada, b_ada, w_in, conv_qkv_w, a_log, dt_bias, gnorm_a, w_ao,
           conv_lru_w, conv_lru_b, lru_wr, lru_br, lru_wi, lru_bi, lru_lambda, w_bo, w_o, norm2_g, w_up,
           conv_ffn_w, conv_ffn_b, w_down, final_g):
    depth, n_heads = a_log.shape
    bp, bs = x_prompt.shape[0], x_sample.shape[0]
    qkv_a = 3 * n_heads * HEAD_DIM
    wts = dict(
        w_qkv=w_in[:, :, :qkv_a].astype(BF16),
        w_rest=w_in[:, :, qkv_a + 2 * n_heads:].astype(BF16),
        w_ab=jnp.pad(w_in[:, :, qkv_a:qkv_a + 2 * n_heads],
                     ((0, 0), (0, 0), (0, LANE - 2 * n_heads))).astype(BF16),
        w_ao=w_ao.astype(BF16), w_bo=w_bo.astype(BF16), w_o=w_o.astype(BF16), w_up=w_up.astype(BF16),
        w_down=w_down.astype(BF16), lru_wr=lru_wr.astype(BF16), lru_wi=lru_wi.astype(BF16),
        norm1_g=norm1_g, conv_qkv_w=conv_qkv_w, a_log=a_log, dt_bias=dt_bias, gnorm_a=gnorm_a,
        conv_lru_w=conv_lru_w, conv_lru_b=conv_lru_b, lru_br=lru_br, lru_bi=lru_bi, lru_lambda=lru_lambda,
        norm2_g=norm2_g, conv_ffn_w=conv_ffn_w, conv_ffn_b=conv_ffn_b, final_g=final_g)

    rows = -(-(bp + bs) // 16) * 16
    c_all = jnp.pad(jnp.concatenate([c_prompt, c_sample], axis=0), ((0, rows - bp - bs), (0, 0)))
    mod_all = _ada_mod(c_all, w_ada, b_ada)

    dt = x_prompt.dtype
    zeros = lambda a: jnp.zeros((depth, bp) + a.shape[2:], a.dtype)
    prompt = _trunk(x_prompt, mod_all[:, :bp], zeros(state_delta).astype(F32), zeros(state_qkv_conv),
                    zeros(state_lru).astype(F32), zeros(state_lru_conv), zeros(state_ffn_conv), True, wts)
    sample = _trunk(x_sample, mod_all[:, bp:bp + bs], state_delta, state_qkv_conv, state_lru, state_lru_conv,
                    state_ffn_conv, False, wts)
    cast = lambda outs: tuple(o.astype(dt) for o in outs)
    return (prompt[0], sample[0]) + cast(prompt[1:]) + cast(sample[1:])
```

```python
import functools
import math

import jax
import jax.numpy as jnp
from jax import lax
from jax.experimental import pallas as pl
from jax.experimental.pallas import tpu as pltpu

F32 = jnp.float32
BF16 = jnp.bfloat16

LANE = 128
SUBLANE = 8
VMEM_LIMIT_BYTES = 56 * 1024 * 1024

EPS = 1e-6
C_LRU = 8.0
HEAD_DIM = 128
DELTA_CHUNK = 128
HALO = SUBLANE


def _params(*sem):
    return pltpu.CompilerParams(dimension_semantics=sem, vmem_limit_bytes=VMEM_LIMIT_BYTES)


def _tile(dim, pref):
    if dim <= pref:
        return dim
    for t in range(pref - pref % LANE, 0, -LANE):
        if dim % t == 0:
            return t
    raise ValueError((dim, pref))


def _silu(x):
    return x * jax.nn.sigmoid(x)


def _softplus(x):
    return jnp.maximum(x, 0.0) + jnp.log1p(jnp.exp(-jnp.abs(x)))


def _bdot(a, b):
    return jnp.dot(a.astype(BF16), b.astype(BF16), preferred_element_type=F32)


def _ada_kernel(c_ref, w_ref, b_ref, o_ref):
    o_ref[0] = _bdot(_silu(c_ref[...]), w_ref[0]) + b_ref[0]


def _ada_mod(c_all, w_ada, b_ada):
    rows, d = c_all.shape
    depth, _, n = w_ada.shape
    tn = _tile(n, 512)
    return pl.pallas_call(
        _ada_kernel,
        out_shape=jax.ShapeDtypeStruct((depth, rows, n), F32),
        grid=(depth, n // tn),
        in_specs=[
            pl.BlockSpec((rows, d), lambda l, j: (0, 0)),
            pl.BlockSpec((1, d, tn), lambda l, j: (l, 0, j)),
            pl.BlockSpec((1, 1, tn), lambda l, j: (l, 0, j)),
        ],
        out_specs=pl.BlockSpec((1, rows, tn), lambda l, j: (l, 0, j)),
        compiler_params=_params("parallel", "parallel"),
        name="ada_mod",
    )(c_all, w_ada, b_ada.reshape(depth, 1, n))


def _norm_kernel(*refs, modulate):
    if modulate:
        x_ref, g_ref, sc_ref, sh_ref, o_ref = refs
    else:
        x_ref, g_ref, o_ref = refs
    x = x_ref[0]
    y = x * lax.rsqrt(jnp.mean(x * x, axis=-1, keepdims=True) + EPS)
    if modulate:
        y = y * (g_ref[...] * (1.0 + sc_ref[0])) + sh_ref[0]
    else:
        y = y * g_ref[...]
    o_ref[0] = y.astype(o_ref.dtype)


def _rms_norm(x, g, sc=None, sh=None, *, out_dtype):
    b, t, d = x.shape
    tt = _tile(t, 512)
    modulate = sc is not None
    row = pl.BlockSpec((1, 1, d), lambda i, j: (i, 0, 0))
    in_specs = [pl.BlockSpec((1, tt, d), lambda i, j: (i, j, 0)), pl.BlockSpec((1, d), lambda i, j: (0, 0))]
    args = [x, g.reshape(1, d)]
    if modulate:
        in_specs += [row, row]
        args += [sc, sh]
    return pl.pallas_call(
        functools.partial(_norm_kernel, modulate=modulate),
        out_shape=jax.ShapeDtypeStruct((b, t, d), out_dtype),
        grid=(b, t // tt),
        in_specs=in_specs,
        out_specs=pl.BlockSpec((1, tt, d), lambda i, j: (i, j, 0)),
        compiler_params=_params("parallel", "parallel"),
        name="rms_norm_mod" if modulate else "rms_norm",
    )(*args)


def _matmul_kernel(*refs, n_extra, nk, epilogue):
    a_ref, b_ref = refs[:2]
    extra = refs[2:2 + n_extra]
    o_ref = refs[2 + n_extra]

    def finish(acc):
        o_ref[...] = epilogue(acc, *[e[...] for e in extra]).astype(o_ref.dtype)

    part = jnp.dot(a_ref[...], b_ref[...], preferred_element_type=F32)
    if nk == 1:
        finish(part)
        return
    acc_ref = refs[3 + n_extra]
    k = pl.program_id(2)

    @pl.when(k == 0)
    def _():
        acc_ref[...] = part

    @pl.when(k > 0)
    def _():
        acc_ref[...] += part

    @pl.when(k == nk - 1)
    def _():
        finish(acc_ref[...])


def _matmul(a, b, layer, *, out_dtype, tm, tn, tk, epilogue=None, extras=(), n_outer=False, name):
    m, k = a.shape
    _, _, n = b.shape
    tm, tn, tk = _tile(m, tm), _tile(n, tn), _tile(k, tk)
    nk = k // tk
    if epilogue is None:
        epilogue = lambda acc: acc
    if n_outer:
        grid = (n // tn, m // tm, nk)
        order = lambda f: (lambda j, i, kk: f(i, j, kk))
    else:
        grid = (m // tm, n // tn, nk)
        order = lambda f: f
    in_specs = [pl.BlockSpec((tm, tk), order(lambda i, j, kk: (i, kk))),
                pl.BlockSpec((None, tk, tn), order(lambda i, j, kk: (layer, kk, j)))]
    for _, shape, imap in extras:
        in_specs.append(pl.BlockSpec(shape, order(functools.partial(lambda i, j, kk, f: f(i, j), f=imap))))
    return pl.pallas_call(
        functools.partial(_matmul_kernel, n_extra=len(extras), nk=nk, epilogue=epilogue),
        out_shape=jax.ShapeDtypeStruct((m, n), out_dtype),
        grid=grid,
        in_specs=in_specs,
        out_specs=pl.BlockSpec((tm, tn), order(lambda i, j, kk: (i, j))),
        scratch_shapes=[pltpu.VMEM((tm, tn), F32)] if nk > 1 else [],
        compiler_params=_params("parallel", "parallel", "arbitrary"),
        name=name,
    )(a, b, *[e[0] for e in extras])


def _mixer_merge_kernel(oa_ref, ob_ref, wa_ref, wb_ref, ga_ref, gb_ref, o_ref):
    y_a = jnp.dot(oa_ref[...], wa_ref[...], preferred_element_type=F32)
    y_b = jnp.dot(ob_ref[...], wb_ref[...], preferred_element_type=F32)
    o_ref[...] = (jax.nn.sigmoid(ga_ref[...]) * y_a + jax.nn.sigmoid(gb_ref[...]) * y_b).astype(o_ref.dtype)


def _mixer_merge(o_a, o_b, w_ao, w_bo, layer, proj, ga_off, gb_off):
    m, ka = o_a.shape
    _, kb = o_b.shape
    n = w_ao.shape[2]
    tm, tn = _tile(m, 512), _tile(math.gcd(n, ga_off, gb_off), 1024)
    return pl.pallas_call(
        _mixer_merge_kernel,
        out_shape=jax.ShapeDtypeStruct((m, n), BF16),
        grid=(n // tn, m // tm),
        in_specs=[
            pl.BlockSpec((tm, ka), lambda j, i: (i, 0)),
            pl.BlockSpec((tm, kb), lambda j, i: (i, 0)),
            pl.BlockSpec((None, ka, tn), lambda j, i: (layer, 0, j)),
            pl.BlockSpec((None, kb, tn), lambda j, i: (layer, 0, j)),
            pl.BlockSpec((tm, tn), lambda j, i: (i, ga_off // tn + j)),
            pl.BlockSpec((tm, tn), lambda j, i: (i, gb_off // tn + j)),
        ],
        out_specs=pl.BlockSpec((tm, tn), lambda j, i: (i, j)),
        compiler_params=_params("parallel", "parallel"),
        name="mixer_merge",
    )(o_a, o_b, w_ao, w_bo, proj, proj)


def _gate_extra(gate, t, tm, tn):
    if tm <= t:
        return (gate, (1, 1, tn), lambda i, j: ((i * tm) // t, 0, j))
    assert tm % t == 0
    return (gate, (tm // t, 1, tn), lambda i, j: (i, 0, j))


def _residual_epilogue_multi(acc, x, gate, *, t):
    nb = gate.shape[0]
    if nb == 1:
        return x + gate[0] * acc
    parts = [x[s * t:(s + 1) * t] + gate[s] * acc[s * t:(s + 1) * t] for s in range(nb)]
    return jnp.concatenate(parts, axis=0)


def _conv_from_scratch(xp_ref, w_ref, rows, base=0):
    kw = w_ref.shape[0]
    y = None
    for j in range(kw):
        term = w_ref[j:j + 1, :] * xp_ref[pl.ds(base + HALO - (kw - 1) + j, rows), :]
        y = term if y is None else y + term
    return y


def _conv_proj_rows(h_ref, st_ref, cw_ref, nb_ref, scr_ref, project, emit, *, seg, nseg, tiles_per_batch, sub):
    i = pl.program_id(1)
    hl = cw_ref.shape[0] - 1
    pos = i % tiles_per_batch
    if nseg == 1:
        if tiles_per_batch == 1:
            scr_ref[HALO - hl:HALO, :] = st_ref[0]
        else:
            @pl.when(pos == 0)
            def _():
                scr_ref[HALO - hl:HALO, :] = st_ref[0]

            @pl.when(pos > 0)
            def _():
                scr_ref[HALO - hl:HALO, :] = scr_ref[HALO + seg - hl:HALO + seg, :]

        def start(r):
            x, aux = project(h_ref[r * sub:(r + 1) * sub, :])
            scr_ref[HALO + r * sub:HALO + (r + 1) * sub, :] = x
            return aux

        n_sub = seg // sub
        aux = start(0)
        for r in range(n_sub):
            aux_next = start(r + 1) if r + 1 < n_sub else None
            emit(slice(r * sub, (r + 1) * sub), _conv_from_scratch(scr_ref, cw_ref, sub, r * sub), aux)
            aux = aux_next

        if tiles_per_batch == 1:
            nb_ref[0] = scr_ref[HALO + seg - hl:HALO + seg, :]
        else:
            @pl.when(pos == tiles_per_batch - 1)
            def _():
                nb_ref[0] = scr_ref[HALO + seg - hl:HALO + seg, :]
        return

    x, aux = project(h_ref[...])
    for s in range(nseg):
        base = s * (HALO + seg)
        rows = slice(s * seg, (s + 1) * seg)
        scr_ref[base + HALO - hl:base + HALO, :] = st_ref[s]
        scr_ref[base + HALO:base + HALO + seg, :] = x[rows]
        emit(rows, _conv_from_scratch(scr_ref, cw_ref, seg, base), None if aux is None else aux[rows])
        nb_ref[s] = scr_ref[base + HALO + seg - hl:base + HALO + seg, :]


def _conv_proj_tiling(m, t, tm_pref, sub_pref):
    if t >= 512:
        tm, nseg = _tile(t, tm_pref), 1
    else:
        tm = _tile(m, max(t, 128))
        nseg = tm // t
    seg = tm // nseg
    tiles_per_batch = max(t // tm, 1)
    assert nseg == 1 or tiles_per_batch == 1
    return tm, nseg, seg, _tile(seg, sub_pref), tiles_per_batch


def _qkv_proj_kernel(h_ref, w_ref, st_ref, cw_ref, o_ref, nb_ref, scr_ref, *, n_qk_tiles, q_tiles, **tiling):
    j = pl.program_id(0)
    is_qk = j < n_qk_tiles
    scale = jnp.where(j < q_tiles, HEAD_DIM ** -0.5, 1.0).astype(F32)

    def project(h):
        return jnp.dot(h, w_ref[...], preferred_element_type=F32), None

    def emit(rows, conv, _):
        y = _silu(conv)
        for hh in range(y.shape[1] // HEAD_DIM):
            hs = slice(hh * HEAD_DIM, (hh + 1) * HEAD_DIM)
            blk = y[:, hs]
            inv = lax.rsqrt(jnp.sum(blk * blk, axis=-1, keepdims=True) + EPS) * scale
            o_ref[rows, hs] = blk * jnp.where(is_qk, inv, 1.0)

    _conv_proj_rows(h_ref, st_ref, cw_ref, nb_ref, scr_ref, project, emit, **tiling)


def _qkv_proj(h, w_qkv, layer, state, cw, *, t, key_a):
    m, d = h.shape
    n = w_qkv.shape[2]
    b = m // t
    kw = cw.shape[0]
    tn = _tile(key_a, 512)
    tm, nseg, seg, sub, tiles_per_batch = _conv_proj_tiling(m, t, 1024, 128)
    state_spec = pl.BlockSpec((nseg, kw - 1, tn), lambda j, i: ((i * tm) // (nseg * t), 0, j))
    return pl.pallas_call(
        functools.partial(_qkv_proj_kernel, n_qk_tiles=2 * key_a // tn, q_tiles=key_a // tn, seg=seg, nseg=nseg,
                          tiles_per_batch=tiles_per_batch, sub=sub),
        out_shape=(jax.ShapeDtypeStruct((m, n), F32), jax.ShapeDtypeStruct((b, kw - 1, n), F32)),
        grid=(n // tn, m // tm),
        in_specs=[
            pl.BlockSpec((tm, d), lambda j, i: (i, 0)),
            pl.BlockSpec((None, d, tn), lambda j, i: (layer, 0, j)),
            state_spec,
            pl.BlockSpec((kw, tn), lambda j, i: (0, j)),
        ],
        out_specs=(pl.BlockSpec((tm, tn), lambda j, i: (i, j)), state_spec),
        scratch_shapes=[pltpu.VMEM((nseg * (HALO + seg), tn), F32)],
        compiler_params=_params("parallel", "arbitrary"),
        name="qkv_proj",
    )(h, w_qkv, state, cw)


def _block_diag(m):
    z = jnp.zeros((m.shape[0], HEAD_DIM), m.dtype)
    return jnp.concatenate([jnp.concatenate([m[:, :HEAD_DIM], z], axis=1),
                            jnp.concatenate([z, m[:, HEAD_DIM:]], axis=1)], axis=0)


def _delta_prep_kernel(q_ref, k_ref, v_ref, ab_ref, alog_ref, dtb_ref, lvl_ref, u_ref, wq_ref, qkk_ref, cd_ref,
                       t_ref, a_ref, rhs_ref, *, n_heads, t_valid):
    L, D, W = DELTA_CHUNK, HEAD_DIM, 2 * HEAD_DIM
    n_pairs = n_heads // 2
    n_levels = lvl_ref.shape[0]
    t = pl.program_id(1)
    ab = ab_ref[0]
    valid = (t * L + lax.broadcasted_iota(jnp.int32, ab.shape, 0)) < t_valid
    g = jnp.where(valid, -jnp.exp(alog_ref[...]) * _softplus(ab + dtb_ref[...]), 0.0)
    beta = jnp.where(valid, jax.nn.sigmoid(ab), 0.0)
    lane = lax.broadcasted_iota(jnp.int32, ab.shape, 1)
    tri = (lax.broadcasted_iota(jnp.int32, (L, L), 0) >= lax.broadcasted_iota(jnp.int32, (L, L), 1)).astype(F32)
    cum = jnp.dot(tri, g, precision=lax.Precision.HIGHEST, preferred_element_type=F32)
    cum_t = cum.T
    ii = lax.broadcasted_iota(jnp.int32, (L, W), 0)
    jj = lax.broadcasted_iota(jnp.int32, (L, W), 1) & (D - 1)
    causal = ii >= jj
    strict = ii > jj
    eye = jnp.where(ii == jj, 1.0, 0.0)
    second = lax.broadcasted_iota(jnp.int32, (1, W), 1) >= D

    def pair_cols(f):
        return jnp.where(second, f(1), f(0))

    for p in range(n_pairs):
        cols = slice(p * W, (p + 1) * W)
        pick = lambda x, idx: jnp.sum(jnp.where(lane == idx, x, 0.0), axis=1, keepdims=True)
        cum_col = pair_cols(lambda e: pick(cum, 2 * p + e))
        b_col = pair_cols(lambda e: pick(beta, n_heads + 2 * p + e))
        cum_row = jnp.concatenate([cum_t[2 * p:2 * p + 1, :], cum_t[2 * p + 1:2 * p + 2, :]], axis=1)
        g_last = cum_col[L - 1:L, :]
        gamma = jnp.where(causal, jnp.exp(jnp.where(causal, cum_col - cum_row, 0.0)), 0.0)
        e_col = jnp.exp(cum_col)

        q, k, v = q_ref[0, :, cols], k_ref[0, :, cols], v_ref[0, :, cols]
        kb = k * b_col
        kq = lax.dot_general(jnp.concatenate([kb, q], axis=0).astype(BF16), _block_diag(k.astype(BF16)),
                             (((1,), (1,)), ((), ())), preferred_element_type=F32)
        a_mat = jnp.where(strict, kq[:L] * gamma, 0.0)
        a_ref[p] = a_mat.astype(BF16)
        t_ref[p] = eye - jnp.where((ii ^ jj) == 1, a_mat, 0.0)
        vb, kbe = v * b_col, kb * e_col
        rhs_ref[p] = jnp.concatenate([vb[:, :D], kbe[:, :D], vb[:, D:], kbe[:, D:]], axis=1)
        wq_ref[0, L:2 * L, cols] = (q * e_col).astype(BF16)
        qkk_ref[0, 0:L, cols] = (kq[L:] * gamma).astype(BF16)
        kd = k * jnp.exp(g_last - cum_col)
        qkk_ref[0, L:2 * L, cols] = jnp.concatenate([kd[:, :D].T, kd[:, D:].T], axis=1).astype(BF16)
        cd_ref[0, 0, :, cols] = jnp.broadcast_to(jnp.exp(g_last), (SUBLANE, W))

    for s in range(1, n_levels):
        for p in range(n_pairs):
            tm = t_ref[p]
            tb = tm.astype(BF16)
            pm = jnp.dot(tb, _block_diag(a_ref[p] * lvl_ref[s]), preferred_element_type=F32)
            qm = jnp.dot(pm.astype(BF16), _block_diag(tb), preferred_element_type=F32)
            t_ref[p] = tm - qm

    for p in range(n_pairs):
        xb = (t_ref[p] - eye).astype(BF16)
        for e in range(2):
            hs = slice((2 * p + e) * D, (2 * p + e + 1) * D)
            rhs = rhs_ref[p, :, e * W:(e + 1) * W]
            sol = rhs + jnp.dot(xb[:, e * D:(e + 1) * D], rhs.astype(BF16), preferred_element_type=F32)
            u_ref[0, :, hs] = sol[:, :D]
            wq_ref[0, 0:L, hs] = sol[:, D:].astype(BF16)


def _level_masks():
    L = DELTA_CHUNK
    ii = jnp.arange(L)[:, None]
    jj = jnp.arange(2 * HEAD_DIM)[None, :] % HEAD_DIM
    return jnp.stack([(((ii >> s) ^ (jj >> s)) == 1) for s in range(L.bit_length() - 1)]).astype(BF16)


def _delta_prep(qkv, ab, a_log, dt_bias, *, n_heads, t_valid):
    b, t, _ = qkv.shape
    L = DELTA_CHUNK
    assert L == HEAD_DIM and n_heads % 2 == 0
    hw = n_heads * HEAD_DIM
    nc = t // L
    lvl = _level_masks()
    pad = lambda a: jnp.pad(a.reshape(1, n_heads), ((0, 0), (0, LANE - n_heads)))
    tok = lambda off: pl.BlockSpec((1, L, hw), lambda i, j: (i, j, off))
    row = pl.BlockSpec((1, LANE), lambda i, j: (0, 0))
    pair_scratch = lambda width, dtype=F32: pltpu.VMEM((n_heads // 2, L, width), dtype)
    return pl.pallas_call(
        functools.partial(_delta_prep_kernel, n_heads=n_heads, t_valid=t_valid),
        out_shape=(
            jax.ShapeDtypeStruct((b, t, hw), F32),
            jax.ShapeDtypeStruct((b, 2 * t, hw), BF16),
            jax.ShapeDtypeStruct((b, 2 * t, hw), BF16),
            jax.ShapeDtypeStruct((b, nc, SUBLANE, hw), F32),
        ),
        grid=(b, nc),
        in_specs=[tok(0), tok(1), tok(2), pl.BlockSpec((1, L, LANE), lambda i, j: (i, j, 0)), row, row,
                  pl.BlockSpec(lvl.shape, lambda i, j: (0, 0, 0))],
        out_specs=(
            pl.BlockSpec((1, L, hw), lambda i, j: (i, j, 0)),
            pl.BlockSpec((1, 2 * L, hw), lambda i, j: (i, j, 0)),
            pl.BlockSpec((1, 2 * L, hw), lambda i, j: (i, j, 0)),
            pl.BlockSpec((1, 1, SUBLANE, hw), lambda i, j: (i, j, 0, 0)),
        ),
        scratch_shapes=[pair_scratch(2 * HEAD_DIM), pair_scratch(2 * HEAD_DIM, BF16), pair_scratch(4 * HEAD_DIM)],
        compiler_params=_params("parallel", "parallel"),
        name="delta_prep",
    )(qkv, qkv, qkv, ab, pad(a_log), pad(dt_bias), lvl)


def _delta_scan_kernel(u_ref, wq_ref, qkk_ref, cd_ref, z_ref, gn_ref, s0_ref, o_ref, so_ref, s_ref,
                       *, hb, nc, rows_out):
    L = DELTA_CHUNK
    t = pl.program_id(2)

    @pl.when(t == 0)
    def _():
        s_ref[...] = s0_ref[0]

    for c in range(nc):
        for hh in range(hb):
            hs = slice(hh * HEAD_DIM, (hh + 1) * HEAD_DIM)
            s = s_ref[hh]
            x1 = jnp.dot(wq_ref[0, 2 * c * L:2 * (c + 1) * L, hs], s.astype(BF16), preferred_element_type=F32)
            v_new = u_ref[0, c * L:(c + 1) * L, hs] - x1[:L]
            x2 = jnp.dot(qkk_ref[0, 2 * c * L:2 * (c + 1) * L, hs], v_new.astype(BF16),
                         preferred_element_type=F32)
            s_ref[hh] = s * cd_ref[0, c, 0:1, hs] + x2[L:]
            o = (x1[L:] + x2[:L])[:rows_out]
            z = z_ref[0, c * rows_out:(c + 1) * rows_out, hs]
            o = o * lax.rsqrt(jnp.mean(o * o, axis=-1, keepdims=True) + EPS) * gn_ref[...] * _silu(z)
            o_ref[0, c * rows_out:(c + 1) * rows_out, hs] = o.astype(o_ref.dtype)

    @pl.when(t == pl.num_programs(2) - 1)
    def _():
        so_ref[0] = s_ref[...]


def _delta_scan(u, wq, qkk, cd, proj, z_off, gnorm, s0, *, t_real):
    b, t, hw = u.shape
    L = DELTA_CHUNK
    n_heads = hw // HEAD_DIM
    hb = min(n_heads, 8)
    hbw = hb * HEAD_DIM
    assert z_off % hbw == 0
    nc = min(t // L, 4)
    rows_out = L if t_real == t else t_real
    assert t_real == t or t == L
    nt = t // (nc * L)
    return pl.pallas_call(
        functools.partial(_delta_scan_kernel, hb=hb, nc=nc, rows_out=rows_out),
        out_shape=(jax.ShapeDtypeStruct((b, t_real, hw), BF16), jax.ShapeDtypeStruct(s0.shape, F32)),
        grid=(b, n_heads // hb, nt),
        in_specs=[
            pl.BlockSpec((1, nc * L, hbw), lambda i, h, j: (i, j, h)),
            pl.BlockSpec((1, 2 * nc * L, hbw), lambda i, h, j: (i, j, h)),
            pl.BlockSpec((1, 2 * nc * L, hbw), lambda i, h, j: (i, j, h)),
            pl.BlockSpec((1, nc, SUBLANE, hbw), lambda i, h, j: (i, j, 0, h)),
            pl.BlockSpec((1, nc * rows_out, hbw), lambda i, h, j: (i, j, z_off // hbw + h)),
            pl.BlockSpec((1, HEAD_DIM), lambda i, h, j: (0, 0)),
            pl.BlockSpec((1, hb, HEAD_DIM, HEAD_DIM), lambda i, h, j: (i, h, 0, 0)),
        ],
        out_specs=(
            pl.BlockSpec((1, nc * rows_out, hbw), lambda i, h, j: (i, j, h)),
            pl.BlockSpec((1, hb, HEAD_DIM, HEAD_DIM), lambda i, h, j: (i, h, 0, 0)),
        ),
        scratch_shapes=[pltpu.VMEM((hb, HEAD_DIM, HEAD_DIM), F32)],
        compiler_params=_params("parallel", "parallel", "arbitrary"),
        name="delta_scan",
    )(u, wq, qkk, cd, proj, gnorm.reshape(1, HEAD_DIM), s0)


def _lru_kernel(x_ref, gate_ref, st_ref, h0_ref, cw_ref, cb_ref, wr_ref, wi_ref, br_ref, bi_ref, lam_ref,
                y_ref, nb_ref, hn_ref, xp_ref, hc_ref, *, tt, first, blk):
    t = pl.program_id(2)
    nt = pl.num_programs(2)
    hl = cw_ref.shape[0] - 1

    @pl.when(t == 0)
    def _():
        xp_ref[HALO - hl:HALO, :] = st_ref[0]
        hc_ref[...] = h0_ref[0]

    @pl.when(t > 0)
    def _():
        xp_ref[HALO - hl:HALO, :] = xp_ref[HALO + tt - hl:HALO + tt, :]

    xp_ref[HALO:HALO + tt, :] = x_ref[0]
    xc = _conv_from_scratch(xp_ref, cw_ref, tt) + cb_ref[...]
    r_parts, i_parts = [], []
    for n in range(xc.shape[1] // blk):
        xn = xc[:, n * blk:(n + 1) * blk].astype(BF16)
        r_parts.append(jnp.dot(xn, wr_ref[n], preferred_element_type=F32))
        i_parts.append(jnp.dot(xn, wi_ref[n], preferred_element_type=F32))
    r = jax.nn.sigmoid(jnp.concatenate(r_parts, axis=1) + br_ref[...])
    gi = jax.nn.sigmoid(jnp.concatenate(i_parts, axis=1) + bi_ref[...])
    log_a = -C_LRU * r * _softplus(-lam_ref[...])
    a = jnp.exp(log_a)
    th = jnp.tanh(log_a)
    mult = jnp.sqrt(-2.0 * th / (1.0 - th))
    if first:
        reset = (lax.broadcasted_iota(jnp.int32, a.shape, 0) == 0) & (t == 0)
        a = jnp.where(reset, 0.0, a)
        mult = jnp.where(reset, 1.0, mult)
    u = mult * gi * xc

    grouped = (tt // SUBLANE, SUBLANE, a.shape[1])
    a, u = a.reshape(grouped), u.reshape(grouped)
    in_group = lax.broadcasted_iota(jnp.int32, grouped, 1)
    s = 1
    while s < SUBLANE:
        keep = in_group >= s
        u = a * jnp.where(keep, pltpu.roll(u, s, axis=1), 0.0) + u
        a = a * jnp.where(keep, pltpu.roll(a, s, axis=1), 1.0)
        s *= 2
    a, u = a.reshape(tt, -1), u.reshape(tt, -1)
    carry = hc_ref[...]
    groups = []
    for r in range(tt // SUBLANE):
        rows = slice(r * SUBLANE, (r + 1) * SUBLANE)
        groups.append(u[rows] + a[rows] * carry)
        carry = groups[-1][SUBLANE - 1:SUBLANE, :]
    h = jnp.concatenate(groups, axis=0)
    hc_ref[...] = carry
    y_ref[0] = (h * jax.nn.gelu(gate_ref[0])).astype(y_ref.dtype)

    @pl.when(t == nt - 1)
    def _():
        nb_ref[0] = xp_ref[HALO + tt - hl:HALO + tt, :]
        hn_ref[0] = h[tt - 1:tt, :]


def _lru(proj, x_off, gate_off, st_conv, h0, cw, cb, wr, wi, layer, br, bi, lam, *, first):
    b, t, _ = proj.shape
    _, nblk, blk, _ = wr.shape
    w = nblk * blk
    tt = _tile(t, 512)
    tc = _tile(w, 512)
    assert x_off % tc == 0 and gate_off % tc == 0 and tc % blk == 0
    kw = cw.shape[0]
    col = lambda a: a.reshape(1, w)
    rowspec = pl.BlockSpec((1, tc), lambda i, c, j: (0, c))
    return pl.pallas_call(
        functools.partial(_lru_kernel, tt=tt, first=first, blk=blk),
        out_shape=(
            jax.ShapeDtypeStruct((b, t, w), BF16),
            jax.ShapeDtypeStruct((b, kw - 1, w), F32),
            jax.ShapeDtypeStruct((b, 1, w), F32),
        ),
        grid=(b, w // tc, t // tt),
        in_specs=[
            pl.BlockSpec((1, tt, tc), lambda i, c, j: (i, j, x_off // tc + c)),
            pl.BlockSpec((1, tt, tc), lambda i, c, j: (i, j, gate_off // tc + c)),
            pl.BlockSpec((1, kw - 1, tc), lambda i, c, j: (i, 0, c)),
            pl.BlockSpec((1, 1, tc), lambda i, c, j: (i, 0, c)),
            pl.BlockSpec((kw, tc), lambda i, c, j: (0, c)),
            rowspec,
            pl.BlockSpec((None, tc // blk, blk, blk), lambda i, c, j: (layer, c, 0, 0)),
            pl.BlockSpec((None, tc // blk, blk, blk), lambda i, c, j: (layer, c, 0, 0)),
            rowspec, rowspec, rowspec,
        ],
        out_specs=(
            pl.BlockSpec((1, tt, tc), lambda i, c, j: (i, j, c)),
            pl.BlockSpec((1, kw - 1, tc), lambda i, c, j: (i, 0, c)),
            pl.BlockSpec((1, 1, tc), lambda i, c, j: (i, 0, c)),
        ),
        scratch_shapes=[pltpu.VMEM((HALO + tt, tc), F32), pltpu.VMEM((1, tc), F32)],
        compiler_params=_params("parallel", "parallel", "arbitrary"),
        name="rg_lru",
    )(proj, proj, st_conv, h0.reshape(b, 1, w), cw, col(cb), wr, wi, col(br), col(bi), col(lam))


def _ffn_up_kernel(h_ref, wg_ref, wv_ref, st_ref, cw_ref, cb_ref, act_ref, nb_ref, scr_ref, **tiling):
    def project(h):
        return (jnp.dot(h, wg_ref[...], preferred_element_type=F32),
                jnp.dot(h, wv_ref[...], preferred_element_type=F32))

    def emit(rows, conv, value):
        act_ref[rows, :] = (jax.nn.gelu(conv + cb_ref[...]) * value).astype(act_ref.dtype)

    _conv_proj_rows(h_ref, st_ref, cw_ref, nb_ref, scr_ref, project, emit, **tiling)


def _ffn_up(h, w_up, layer, state, cw, cb, *, t):
    m, d = h.shape
    f = w_up.shape[2] // 2
    b = m // t
    kw = cw.shape[0]
    tn = _tile(f, 512)
    tm, nseg, seg, sub, tiles_per_batch = _conv_proj_tiling(m, t, 1024, 256)
    return pl.pallas_call(
        functools.partial(_ffn_up_kernel, seg=seg, nseg=nseg, tiles_per_batch=tiles_per_batch, sub=sub),
        out_shape=(jax.ShapeDtypeStruct((m, f), BF16), jax.ShapeDtypeStruct((b, kw - 1, f), F32)),
        grid=(f // tn, m // tm),
        in_specs=[
            pl.BlockSpec((tm, d), lambda j, i: (i, 0)),
            pl.BlockSpec((None, d, tn), lambda j, i: (layer, 0, j)),
            pl.BlockSpec((None, d, tn), lambda j, i: (layer, 0, f // tn + j)),
            pl.BlockSpec((nseg, kw - 1, tn), lambda j, i: ((i * tm) // (nseg * t), 0, j)),
            pl.BlockSpec((kw, tn), lambda j, i: (0, j)),
            pl.BlockSpec((1, tn), lambda j, i: (0, j)),
        ],
        out_specs=(
            pl.BlockSpec((tm, tn), lambda j, i: (i, j)),
            pl.BlockSpec((nseg, kw - 1, tn), lambda j, i: ((i * tm) // (nseg * t), 0, j)),
        ),
        scratch_shapes=[pltpu.VMEM((nseg * (HALO + seg), tn), F32)],
        compiler_params=_params("parallel", "arbitrary"),
        name="ffn_up",
    )(h, w_up, w_up, state, cw, cb.reshape(1, f))


def _pad_rows(a, rows):
    return a if a.shape[1] == rows else jnp.pad(a, ((0, 0), (0, rows - a.shape[1]), (0, 0)))


def _trunk(x, mod_all, st_delta, st_qkv, st_lru, st_lruconv, st_ffn, first, wts):
    b, t, d = x.shape
    m = b * t
    n_heads = wts["a_log"].shape[1]
    key_a = n_heads * HEAD_DIM
    qkv_a = 3 * key_a
    w_lru = wts["lru_lambda"].shape[1]
    depth = wts["a_log"].shape[0]
    z_off = 0
    xb_off = z_off + key_a
    gb_off = xb_off + w_lru
    ga_off = gb_off + w_lru
    gbl_off = ga_off + d
    t_pad = -(-t // DELTA_CHUNK) * DELTA_CHUNK
    tm_big = 1024
    outs = {k: [] for k in ("delta", "qkv", "lru", "lruconv", "ffn")}

    for l in range(depth):
        sh1, sc1, gt1, sh2, sc2, gt2 = [c.reshape(b, 1, d) for c in jnp.split(mod_all[l], 6, axis=-1)]
        h = _rms_norm(x, wts["norm1_g"][l], sc1, sh1, out_dtype=BF16).reshape(m, d)
        qkv_c, new_qkv = _qkv_proj(h, wts["w_qkv"], l, st_qkv[l], wts["conv_qkv_w"][l], t=t, key_a=key_a)
        proj = _matmul(h, wts["w_rest"], l, out_dtype=F32, tm=tm_big, tn=1024, tk=d, name="in_proj")
        ab = _matmul(h, wts["w_ab"], l, out_dtype=F32, tm=tm_big, tn=LANE, tk=d, name="in_proj_gates")
        proj3 = proj.reshape(b, t, -1)

        qkv_c = qkv_c.reshape(b, t, qkv_a)
        u, wq, qkk, cd = _delta_prep(_pad_rows(qkv_c, t_pad), _pad_rows(ab.reshape(b, t, LANE), t_pad),
                                     wts["a_log"][l], wts["dt_bias"][l], n_heads=n_heads, t_valid=t)
        o_a, new_delta = _delta_scan(u, wq, qkk, cd, proj3, z_off, wts["gnorm_a"][l], st_delta[l], t_real=t)

        y_lru, new_lruconv, new_lru = _lru(
            proj3, xb_off, gb_off, st_lruconv[l], st_lru[l], wts["conv_lru_w"][l], wts["conv_lru_b"][l],
            wts["lru_wr"], wts["lru_wi"], l, wts["lru_br"][l], wts["lru_bi"][l], wts["lru_lambda"][l],
            first=first)

        mixed = _mixer_merge(o_a.reshape(m, key_a), y_lru.reshape(m, w_lru), wts["w_ao"], wts["w_bo"], l,
                             proj, ga_off, gbl_off)
        tm, tn = _tile(m, 1024), _tile(d, 1024)
        x2 = x.reshape(m, d)
        x2 = _matmul(mixed, wts["w_o"], l, out_dtype=F32, tm=tm, tn=tn, tk=d, n_outer=True,
                     epilogue=functools.partial(_residual_epilogue_multi, t=t),
                     extras=[(x2, (tm, tn), lambda i, j: (i, j)), _gate_extra(gt1, t, tm, tn)], name="proj_o")

        h = _rms_norm(x2.reshape(b, t, d), wts["norm2_g"][l], sc2, sh2, out_dtype=BF16).reshape(m, d)
        act, new_ffn = _ffn_up(h, wts["w_up"], l, st_ffn[l], wts["conv_ffn_w"][l], wts["conv_ffn_b"][l], t=t)
        tm, tn = _tile(m, 512), _tile(d, 512)
        x2 = _matmul(act, wts["w_down"], l, out_dtype=F32, tm=tm, tn=tn, tk=act.shape[1], n_outer=True,
                     epilogue=functools.partial(_residual_epilogue_multi, t=t),
                     extras=[(x2, (tm, tn), lambda i, j: (i, j)), _gate_extra(gt2, t, tm, tn)], name="ffn_down")
        x = x2.reshape(b, t, d)

        outs["delta"].append(new_delta)
        outs["qkv"].append(new_qkv)
        outs["lru"].append(new_lru.reshape(b, w_lru))
        outs["lruconv"].append(new_lruconv)
        outs["ffn"].append(new_ffn)

    y = _rms_norm(x, wts["final_g"], out_dtype=F32)
    return (y,) + tuple(jnp.stack(outs[k]) for k in ("delta", "qkv", "lru", "lruconv", "ffn"))


def kernel(x_prompt, x_sample, state_delta, state_qkv_conv, state_lru, state_lru_conv, state_ffn_conv,
           c_prompt, c_sample, norm1_g, w_ada, b_ada, w_in, conv_qkv_w, a_log, dt_bias, gnorm_a, w_ao,
           conv_lru_w, conv_lru_b, lru_wr, lru_br, lru_wi, lru_bi, lru_lambda, w_bo, w_o, norm2_g, w_up,
           conv_ffn_w, conv_ffn_b, w_down, final_g):
    depth, n_heads = a_log.shape
    bp, bs = x_prompt.shape[0], x_sample.shape[0]
    qkv_a = 3 * n_heads * HEAD_DIM
    wts = dict(
        w_qkv=w_in[:, :, :qkv_a].astype(BF16),
        w_rest=w_in[:, :, qkv_a + 2 * n_heads:].astype(BF16),
        w_ab=jnp.pad(w_in[:, :, qkv_a:qkv_a + 2 * n_heads],
                     ((0, 0), (0, 0), (0, LANE - 2 * n_heads))).astype(BF16),
        w_ao=w_ao.astype(BF16), w_bo=w_bo.astype(BF16), w_o=w_o.astype(BF16), w_up=w_up.astype(BF16),
        w_down=w_down.astype(BF16), lru_wr=lru_wr.astype(BF16), lru_wi=lru_wi.astype(BF16),
        norm1_g=norm1_g, conv_qkv_w=conv_qkv_w, a_log=a_log, dt_bias=dt_bias, gnorm_a=gnorm_a,
        conv_lru_w=conv_lru_w, conv_lru_b=conv_lru_b, lru_br=lru_br, lru_bi=lru_bi, lru_lambda=lru_lambda,
        norm2_g=norm2_g, conv_ffn_w=conv_ffn_w, conv_ffn_b=conv_ffn_b, final_g=final_g)

    rows = -(-(bp + bs) // 16) * 16
    c_all = jnp.pad(jnp.concatenate([c_prompt, c_sample], axis=0), ((0, rows - bp - bs), (0, 0)))
    mod_all = _ada_mod(c_all, w_ada, b_ada)

    dt = x_prompt.dtype
    zeros = lambda a: jnp.zeros((depth, bp) + a.shape[2:], a.dtype)
    prompt = _trunk(x_prompt, mod_all[:, :bp], zeros(state_delta).astype(F32), zeros(state_qkv_conv),
                    zeros(state_lru).astype(F32), zeros(state_lru_conv), zeros(state_ffn_conv), True, wts)
    sample = _trunk(x_sample, mod_all[:, bp:bp + bs], state_delta, state_qkv_conv, state_lru, state_lru_conv,
                    state_ffn_conv, False, wts)
    cast = lambda outs: tuple(o.astype(dt) for o in outs)
    return (prompt[0], sample[0]) + cast(prompt[1:]) + cast(sample[1:])
```
